```python
import math
import jax, jax.numpy as jnp
from jax import lax
import numpy as np

D_MODEL = 1024
BATCH = 1
SEQ = 16384
DEPTH = 4

CHUNK = 128
D_MIX = D_MODEL
GMLP_HEADS = 4
GMLP_HEAD_DIM = 64
GMLP_W = GMLP_HEADS * GMLP_HEAD_DIM
SSD_HEADS = 6
SSD_HEAD_DIM = 64
SSD_W = SSD_HEADS * SSD_HEAD_DIM
SSD_GROUPS = 2
SSD_STATE = 64
SSD_CONV = 4
SSD_XBC = SSD_W + 2 * SSD_GROUPS * SSD_STATE
MLA_HEADS = 6
MLA_NOPE = 64
MLA_ROPE = 32
MLA_V = 64
MLA_W = MLA_HEADS * MLA_V
MLA_Q_RANK = 384
MLA_KV_RANK = 256
ROPE_BASE = 10000.0
D_FF = 4 * D_MODEL
EPS = 1e-6
IN_WIDTHS = (2 * GMLP_W, SSD_W, SSD_XBC, SSD_HEADS, MLA_Q_RANK, MLA_KV_RANK, MLA_ROPE)
D_IN = sum(IN_WIDTHS)

kernel_name = "hybrid_gmlp_ssd_mla_block"


def rmsnorm(x, g):
    xf = x.astype(jnp.float32)
    var = jnp.mean(xf * xf, axis=-1, keepdims=True)
    return (xf * lax.rsqrt(var + EPS)).astype(x.dtype) * g


def split_cols(proj):
    idx, acc = [], 0
    for w in IN_WIDTHS[:-1]:
        acc += w
        idx.append(acc)
    return jnp.split(proj, idx, axis=-1)


def rope_tables(S):
    half = MLA_ROPE // 2
    pos = jnp.arange(S, dtype=jnp.float32)
    inv_freq = jnp.power(ROPE_BASE, -jnp.arange(half, dtype=jnp.float32) / half)
    ang = pos[:, None] * inv_freq[None, :]
    return jnp.cos(ang), jnp.sin(ang)


def apply_rope(x, cos, sin):
    c = cos[None, :, None, :].astype(x.dtype)
    s = sin[None, :, None, :].astype(x.dtype)
    x1, x2 = jnp.split(x, 2, axis=-1)
    return jnp.concatenate([x1 * c - x2 * s, x2 * c + x1 * s], axis=-1)


def gmlp_mixer(uv, v_norm_g, w_s, b_s):
    Bsz, S, _ = uv.shape
    nc = S // CHUNK
    uv = jax.nn.gelu(uv)
    u, v = jnp.split(uv, 2, axis=-1)
    v = rmsnorm(v, v_norm_g).reshape(Bsz, nc, CHUNK, GMLP_HEADS, GMLP_HEAD_DIM)
    causal = jnp.tril(jnp.ones((CHUNK, CHUNK), dtype=w_s.dtype))
    mixed = jnp.einsum('hts,bcshd->bcthd', w_s * causal, v)
    mixed = mixed + b_s.T[None, None, :, :, None]
    return u * mixed.reshape(Bsz, S, GMLP_W)


def causal_depthwise_conv(x, w, b):
    K, C = w.shape
    out = lax.conv_general_dilated(
        x, w[:, None, :], window_strides=(1,), padding=((K - 1, 0),),
        dimension_numbers=('NWC', 'WIO', 'NWC'), feature_group_count=C)
    return out + b


def ssd_mixer(z, xbc, dt_raw, conv_w, conv_b, dt_bias, a_log, d_skip, norm_g):
    Bsz, S, _ = xbc.shape
    nc = S // CHUNK
    hpg = SSD_HEADS // SSD_GROUPS
    xbc = jax.nn.silu(causal_depthwise_conv(xbc, conv_w, conv_b))
    xs, Bm, Cm = jnp.split(xbc, [SSD_W, SSD_W + SSD_GROUPS * SSD_STATE], axis=-1)
    xs_c = xs.reshape(Bsz, nc, CHUNK, SSD_HEADS, SSD_HEAD_DIM)
    B_c = Bm.reshape(Bsz, nc, CHUNK, SSD_GROUPS, SSD_STATE)
    C_c = Cm.reshape(Bsz, nc, CHUNK, SSD_GROUPS, SSD_STATE)
    dt = jax.nn.softplus((dt_raw + dt_bias).astype(jnp.float32))
    A = -jnp.exp(a_log.astype(jnp.float32))
    dt_h = dt.reshape(Bsz, nc, CHUNK, SSD_HEADS).transpose(0, 1, 3, 2)
    a_h = jnp.cumsum(dt_h * A[None, None, :, None], axis=-1)
    causal = jnp.tril(jnp.ones((CHUNK, CHUNK), dtype=bool))
    seg = a_h[..., :, None] - a_h[..., None, :]
    decay = jnp.exp(jnp.where(causal, seg, -jnp.inf))
    cb = jnp.einsum('bctgn,bcsgn->bcgts', C_c, B_c)
    cb_h = jnp.repeat(cb, hpg, axis=2)
    w = cb_h * decay * dt_h[:, :, :, None, :]
    y_diag = jnp.einsum('bchts,bcshp->bcthp', w, xs_c)
    B_h = jnp.repeat(B_c, hpg, axis=3)
    C_h = jnp.repeat(C_c, hpg, axis=3)
    decay_to_end = jnp.exp(a_h[..., -1:] - a_h) * dt_h
    states = jnp.einsum('bchs,bcshn,bcshp->bchpn', decay_to_end, B_h, xs_c)
    chunk_decay = jnp.exp(a_h[..., -1])

    def step(h, inp):
        dec, st = inp
        return dec[:, :, None, None] * h + st, h

    h0 = jnp.zeros((Bsz, SSD_HEADS, SSD_HEAD_DIM, SSD_STATE), states.dtype)
    _, h_prev = lax.scan(step, h0, (chunk_decay.transpose(1, 0, 2), states.transpose(1, 0, 2, 3, 4)))
    h_prev = h_prev.transpose(1, 0, 2, 3, 4)
    y_off = jnp.einsum('bcthn,bchpn,bcht->bcthp', C_h, h_prev, jnp.exp(a_h))
    y = y_diag + y_off + xs_c * d_skip[:, None]
    y = y.reshape(Bsz, S, SSD_W).astype(z.dtype)
    yg = (y * jax.nn.silu(z)).reshape(Bsz, S, SSD_GROUPS, SSD_W // SSD_GROUPS)
    return rmsnorm(yg, norm_g.reshape(SSD_GROUPS, -1)).reshape(Bsz, S, SSD_W)


def causal_block_attention(q, k, v, scale):
    Bsz, S, H, Dqk = q.shape
    nq = S // CHUNK
    qb = q.reshape(Bsz, nq, CHUNK, H, Dqk).transpose(1, 0, 3, 2, 4)
    k_pos = jnp.arange(S)

    def one_block(args):
        q_blk, blk = args
        s = jnp.einsum('bhqd,bkhd->bhqk', q_blk, k).astype(jnp.float32) * scale
        q_pos = blk * CHUNK + jnp.arange(CHUNK)
        s = jnp.where(k_pos[None, :] <= q_pos[:, None], s, -jnp.inf)
        p = jax.nn.softmax(s, axis=-1).astype(v.dtype)
        return jnp.einsum('bhqk,bkhd->bqhd', p, v)

    out = lax.map(one_block, (qb, jnp.arange(nq)))
    return out.transpose(1, 0, 2, 3, 4).reshape(Bsz, S, H, v.shape[-1])


def mla_mixer(c_q, c_kv, k_rope_raw, q_norm_g, w_qb, kv_norm_g, w_kvb, cos, sin):
    Bsz, S, _ = c_q.shape
    q = (rmsnorm(c_q, q_norm_g) @ w_qb).reshape(Bsz, S, MLA_HEADS, MLA_NOPE + MLA_ROPE)
    kv = (rmsnorm(c_kv, kv_norm_g) @ w_kvb).reshape(Bsz, S, MLA_HEADS, MLA_NOPE + MLA_V)
    q_nope, q_rope = jnp.split(q, [MLA_NOPE], axis=-1)
    k_nope, v = jnp.split(kv, [MLA_NOPE], axis=-1)
    q_rope = apply_rope(q_rope, cos, sin)
    k_rope = apply_rope(k_rope_raw[:, :, None, :], cos, sin)
    q = jnp.concatenate([q_nope, q_rope], axis=-1)
    k = jnp.concatenate([k_nope, jnp.broadcast_to(k_rope, (Bsz, S, MLA_HEADS, MLA_ROPE))], axis=-1)
    scale = 1.0 / math.sqrt(MLA_NOPE + MLA_ROPE)
    out = causal_block_attention(q, k, v, scale)
    return out.reshape(Bsz, S, MLA_W)


def squared_relu_mlp(h, w1, w2):
    return jnp.square(jax.nn.relu(h @ w1)) @ w2


def setup_inputs(seed: int = 0) -> dict:
    key = jax.random.key(seed)
    ks = jax.random.split(key, 21)

    def nrm(k, shape, scale):
        return jax.random.normal(k, shape, jnp.float32) * scale

    def gain(k, shape):
        return 1.0 + 0.02 * jax.random.normal(k, shape, jnp.float32)

    dt0 = jnp.exp(jax.random.uniform(ks[8], (DEPTH, SSD_HEADS), jnp.float32,
                                     math.log(1e-3), math.log(1e-1)))
    dt_bias = dt0 + jnp.log(-jnp.expm1(-dt0))
    a_log = jnp.log(jax.random.uniform(ks[9], (DEPTH, SSD_HEADS), jnp.float32, 1.0, 16.0))
    return {
        "x": nrm(ks[0], (BATCH, SEQ, D_MODEL), 1.0),
        "norm_mix_g": gain(ks[1], (DEPTH, D_MODEL)),
        "w_in": nrm(ks[2], (DEPTH, D_MODEL, D_IN), D_MODEL ** -0.5),
        "gmlp_v_norm_g": gain(ks[3], (DEPTH, GMLP_W)),
        "gmlp_w_s": nrm(ks[4], (DEPTH, GMLP_HEADS, CHUNK, CHUNK), CHUNK ** -0.5),
        "gmlp_b_s": gain(ks[5], (DEPTH, GMLP_HEADS, CHUNK)),
        "ssd_conv_w": nrm(ks[6], (DEPTH, SSD_CONV, SSD_XBC), SSD_CONV ** -0.5),
        "ssd_conv_b": nrm(ks[7], (DEPTH, SSD_XBC), 0.02),
        "ssd_dt_bias": dt_bias,
        "ssd_a_log": a_log,
        "ssd_d": 1.0 + 0.1 * jax.random.normal(ks[10], (DEPTH, SSD_HEADS), jnp.float32),
        "ssd_norm_g": gain(ks[11], (DEPTH, SSD_W)),
        "mla_q_norm_g": gain(ks[12], (DEPTH, MLA_Q_RANK)),
        "mla_w_qb": nrm(ks[13], (DEPTH, MLA_Q_RANK, MLA_HEADS * (MLA_NOPE + MLA_ROPE)), MLA_Q_RANK ** -0.5),
        "mla_kv_norm_g": gain(ks[14], (DEPTH, MLA_KV_RANK)),
        "mla_w_kvb": nrm(ks[15], (DEPTH, MLA_KV_RANK, MLA_HEADS * (MLA_NOPE + MLA_V)), MLA_KV_RANK ** -0.5),
        "w_out": nrm(ks[16], (DEPTH, D_MIX, D_MODEL), D_MIX ** -0.5),
        "norm_mlp_g": gain(ks[17], (DEPTH, D_MODEL)),
        "mlp_w1": nrm(ks[18], (DEPTH, D_MODEL, D_FF), D_MODEL ** -0.5),
        "mlp_w2": nrm(ks[19], (DEPTH, D_FF, D_MODEL), D_FF ** -0.5),
        "final_norm_g": gain(ks[20], (D_MODEL,)),
    }


def reference(x, norm_mix_g, w_in, gmlp_v_norm_g, gmlp_w_s, gmlp_b_s, ssd_conv_w, ssd_conv_b,
              ssd_dt_bias, ssd_a_log, ssd_d, ssd_norm_g, mla_q_norm_g, mla_w_qb, mla_kv_norm_g,
              mla_w_kvb, w_out, norm_mlp_g, mlp_w1, mlp_w2, final_norm_g):
    S = x.shape[1]
    cos, sin = rope_tables(S)
    for l in range(DEPTH):
        h = rmsnorm(x, norm_mix_g[l])
        uv, z, xbc, dt_raw, c_q, c_kv, k_rope = split_cols(h @ w_in[l])
        y_a = gmlp_mixer(uv, gmlp_v_norm_g[l], gmlp_w_s[l], gmlp_b_s[l])
        y_b = ssd_mixer(z, xbc, dt_raw, ssd_conv_w[l], ssd_conv_b[l], ssd_dt_bias[l],
                        ssd_a_log[l], ssd_d[l], ssd_norm_g[l])
        y_c = mla_mixer(c_q, c_kv, k_rope, mla_q_norm_g[l], mla_w_qb[l], mla_kv_norm_g[l],
                        mla_w_kvb[l], cos, sin)
        mix = jnp.concatenate([y_a, y_b, y_c], axis=-1)
        x = x + mix @ w_out[l]
        x = x + squared_relu_mlp(rmsnorm(x, norm_mlp_g[l]), mlp_w1[l], mlp_w2[l])
    return rmsnorm(x, final_norm_g)
```

```python
import functools
import math

import numpy as np
import jax
import jax.numpy as jnp
from jax import lax
from jax.experimental import pallas as pl
from jax.experimental.pallas import tpu as pltpu

F32 = jnp.float32
BF16 = jnp.bfloat16
HIGHEST = lax.Precision.HIGHEST

D_MODEL = 1024
DEPTH = 4
CHUNK = 128
GMLP_HEADS = 4
GMLP_HEAD_DIM = 64
GMLP_W = GMLP_HEADS * GMLP_HEAD_DIM
SSD_HEADS = 6
SSD_HEAD_DIM = 64
SSD_W = SSD_HEADS * SSD_HEAD_DIM
SSD_GROUPS = 2
SSD_STATE = 64
SSD_CONV = 4
SSD_BC = SSD_GROUPS * SSD_STATE
SSD_XBC = SSD_W + 2 * SSD_BC
MLA_HEADS = 6
MLA_NOPE = 64
MLA_ROPE = 32
MLA_V = 64
MLA_W = MLA_HEADS * MLA_V
MLA_Q_RANK = 384
MLA_KV_RANK = 256
ROPE_BASE = 10000.0
D_FF = 4 * D_MODEL
EPS = 1e-6

LANES = 128
HEAD_SLAB = LANES
MLA_SLAB_W = MLA_HEADS * HEAD_SLAB
V_ONES_LANE = MLA_V
ROPE_LANE0 = MLA_NOPE
VMEM_LIMIT_V7X = 56 * 1024 * 1024

_OFF_UV = 0
_OFF_Z = _OFF_UV + 2 * GMLP_W
_OFF_XBC = _OFF_Z + SSD_W
_OFF_CQ = _OFF_XBC + SSD_XBC
_OFF_CKV = _OFF_CQ + MLA_Q_RANK
_OFF_DT = _OFF_CKV + MLA_KV_RANK
_OFF_KR = _OFF_DT + LANES
D_IN_PAD = _OFF_KR + LANES

TM_PROJ = 512
TM_MLP = 1024
T_ATT = 512
FF_CHUNK = 1024


def _rms(x, g):
    var = jnp.mean(x * x, axis=-1, keepdims=True)
    return (x * lax.rsqrt(var + EPS)) * g


def _sigmoid(x):
    return 1.0 / (1.0 + jnp.exp(-x))


def _lane_group(width, group_width):
    shift = group_width.bit_length() - 1
    assert 1 << shift == group_width
    return lax.shift_right_logical(lax.broadcasted_iota(jnp.int32, (1, width), 1), shift)


def _dot(a, b, precision=None):
    return jnp.dot(a, b, preferred_element_type=F32, precision=precision)


def _dot_nt(a, b):
    return lax.dot_general(a, b, (((1,), (1,)), ((), ())), preferred_element_type=F32)


def _inproj_kernel(x_ref, g_ref, w_ref, uv_ref, z_ref, xbc_ref, cq_ref, ckv_ref, dt_ref, kr_ref):
    h = _rms(x_ref[...], g_ref[...]).astype(BF16)

    def seg(a, b):
        return _dot(h, w_ref[:, a:b])

    uv_ref[...] = seg(_OFF_UV, _OFF_Z).astype(BF16)
    z_ref[...] = seg(_OFF_Z, _OFF_XBC).astype(BF16)
    xbc_ref[...] = seg(_OFF_XBC, _OFF_CQ).astype(BF16)
    cq_ref[...] = seg(_OFF_CQ, _OFF_CKV).astype(BF16)
    ckv_ref[...] = seg(_OFF_CKV, _OFF_DT).astype(BF16)
    dt_ref[...] = seg(_OFF_DT, _OFF_KR)
    kr_ref[...] = seg(_OFF_KR, D_IN_PAD)


def _inproj(x, g, w, l):
    S = x.shape[0]
    tm = TM_PROJ
    widths = (2 * GMLP_W, SSD_W, SSD_XBC, MLA_Q_RANK, MLA_KV_RANK, LANES, LANES)
    dtypes = (BF16, BF16, BF16, BF16, BF16, F32, F32)
    return pl.pallas_call(
        _inproj_kernel,
        grid=(S // tm,),
        in_specs=[
            pl.BlockSpec((tm, D_MODEL), lambda i: (i, 0)),
            pl.BlockSpec((None, 1, D_MODEL), lambda i: (l, 0, 0)),
            pl.BlockSpec((None, D_MODEL, D_IN_PAD), lambda i: (l, 0, 0)),
        ],
        out_specs=[pl.BlockSpec((tm, w_), lambda i: (i, 0)) for w_ in widths],
        out_shape=[jax.ShapeDtypeStruct((S, w_), d_) for w_, d_ in zip(widths, dtypes)],
        compiler_params=pltpu.CompilerParams(
            dimension_semantics=("arbitrary",), vmem_limit_bytes=VMEM_LIMIT_V7X),
        name="inproj",
    )(x, g, w)


def _mixab_kernel(uv_ref, z_ref, xbc_ref, dt_ref, vg_ref, ws_ref, bsf_ref, cw_ref, cb_ref,
                  dtb_ref, alog_ref, dsk_ref, ng_ref, e_ref, g_ref, gt_ref, bd_ref,
                  ya_ref, yb_ref, state_ref, tail_ref):
    L = CHUNK

    @pl.when(pl.program_id(0) == 0)
    def _():
        state_ref[...] = jnp.zeros_like(state_ref)
        tail_ref[...] = jnp.zeros_like(tail_ref)

    row = lax.broadcasted_iota(jnp.int32, (L, L), 0)
    col = lax.broadcasted_iota(jnp.int32, (L, L), 1)
    causal = col <= row
    tril = causal.astype(F32)

    uv = uv_ref[...].astype(F32)
    guv = 0.5 * uv * (1.0 + jnp.tanh(math.sqrt(2.0 / math.pi) * (uv + 0.044715 * (uv * uv * uv))))
    u = guv[:, :GMLP_W]
    vn = _rms(guv[:, GMLP_W:], vg_ref[...])
    head_a = _lane_group(GMLP_W, GMLP_HEAD_DIM)
    mixed = bsf_ref[...]
    for h in range(GMLP_HEADS):
        wm = (ws_ref[h] * tril).astype(BF16)
        vm = (vn * (head_a == h).astype(F32)).astype(BF16)
        mixed = mixed + _dot(wm, vm)
    ya_ref[...] = (u * mixed).astype(BF16)

    x = xbc_ref[...].astype(F32)
    xfull = jnp.concatenate([tail_ref[...], x], axis=0)
    conv = cb_ref[...] + cw_ref[SSD_CONV - 1:SSD_CONV, :] * x
    for k in range(SSD_CONV - 1):
        shifted = pltpu.roll(xfull, SSD_CONV - 1 - k, 0)[8:8 + L]
        conv = conv + cw_ref[k:k + 1, :] * shifted
    tail_ref[...] = x[L - 8:L]
    xc = conv * _sigmoid(conv)
    xs = xc[:, :SSD_W]
    Bm = xc[:, SSD_W:SSD_W + SSD_BC]
    Cm = xc[:, SSD_W + SSD_BC:]

    dtr = dt_ref[...] + dtb_ref[...]
    dt = jnp.maximum(dtr, 0.0) + jnp.log1p(jnp.exp(-jnp.abs(dtr)))
    a_neg = -jnp.exp(alog_ref[...])
    a_col = _dot(tril, dt * a_neg, HIGHEST)
    a_row = a_col.T
    dt_row = dt.T

    Bb = Bm.astype(BF16)
    Cb = Cm.astype(BF16)
    group_bc = _lane_group(SSD_BC, SSD_STATE)
    head_b = _lane_group(SSD_W, SSD_HEAD_DIM)
    hpg = SSD_HEADS // SSD_GROUPS
    y = xs * dsk_ref[...]
    for g in range(SSD_GROUPS):
        cg = (Cm * (group_bc == g).astype(F32)).astype(BF16)
        cbg = _dot_nt(cg, Bb)
        for hh in range(hpg):
            h = g * hpg + hh
            seg = a_col[:, h:h + 1] - a_row[h:h + 1, :]
            decay = jnp.exp(jnp.where(causal, seg, -jnp.inf))
            w = (cbg * decay * dt_row[h:h + 1, :]).astype(BF16)
            xm = (xs * (head_b == h).astype(F32)).astype(BF16)
            y = y + _dot(w, xm)

    ea = jnp.exp(a_col)
    dte = jnp.exp(a_col[L - 1:L, :] - a_col) * dt
    ea_w = _dot(ea, e_ref[...], HIGHEST)
    dte_w = _dot(dte, e_ref[...], HIGHEST)
    s_prev = state_ref[...]
    y = y + _dot(Cb, s_prev.astype(BF16)) * ea_w
    xsd = (xs * dte_w).astype(BF16)
    bt = Bm.T.astype(BF16)
    state_ref[...] = ea_w[L - 1:L, :] * s_prev + bd_ref[...] * _dot(bt, xsd)

    zf = z_ref[...].astype(F32)
    yg = y * (zf * _sigmoid(zf))
    ssq = _dot(yg * yg, g_ref[...], HIGHEST)
    inv = lax.rsqrt(ssq * (1.0 / (SSD_W // SSD_GROUPS)) + EPS)
    yb_ref[...] = (yg * _dot(inv, gt_ref[...], HIGHEST) * ng_ref[...]).astype(BF16)


def _mixab(uv, z, xbc, dt, p, l):
    S = uv.shape[0]
    L = CHUNK

    def rows(w_):
        return pl.BlockSpec((L, w_), lambda i: (i, 0))

    def per_layer(*shape):
        return pl.BlockSpec((None,) + shape, lambda i: (l,) + (0,) * len(shape))

    def const(*shape):
        return pl.BlockSpec(shape, lambda i: (0,) * len(shape))

    return pl.pallas_call(
        _mixab_kernel,
        grid=(S // L,),
        in_specs=[
            rows(2 * GMLP_W), rows(SSD_W), rows(SSD_XBC), rows(LANES),
            per_layer(1, GMLP_W), per_layer(GMLP_HEADS, L, L), per_layer(L, GMLP_W),
            per_layer(SSD_CONV, SSD_XBC), per_layer(1, SSD_XBC),
            per_layer(1, LANES), per_layer(1, LANES), per_layer(1, SSD_W), per_layer(1, SSD_W),
            const(LANES, SSD_W), const(SSD_W, LANES), const(LANES, SSD_W), const(SSD_BC, SSD_W),
        ],
        out_specs=[rows(GMLP_W), rows(SSD_W)],
        out_shape=[jax.ShapeDtypeStruct((S, GMLP_W), BF16), jax.ShapeDtypeStruct((S, SSD_W), BF16)],
        scratch_shapes=[pltpu.VMEM((SSD_BC, SSD_W), F32), pltpu.VMEM((8, SSD_XBC), F32)],
        compiler_params=pltpu.CompilerParams(
            dimension_semantics=("arbitrary",), vmem_limit_bytes=VMEM_LIMIT_V7X),
        name="mixab",
    )(uv, z, xbc, dt, p["vg"], p["ws"], p["bsf"], p["cw"], p["cb"], p["dtb"], p["alog"],
      p["dsk"], p["ng"], p["expand_heads"], p["group_sum"], p["group_expand"], p["state_mask"])


def _mlaproj_kernel(cq_ref, ckv_ref, kr_ref, qg_ref, kvg_ref, wq_ref, wk_ref, wv_ref,
                    ct_ref, s1_ref, s2_ref, q_out, k_out, v_out):
    def rope(x, n):
        def wide(t):
            return t if n == 1 else jnp.concatenate([t] * n, axis=1)
        w = n * HEAD_SLAB
        up = pltpu.roll(x, w - MLA_ROPE // 2, 1)
        down = pltpu.roll(x, MLA_ROPE // 2, 1)
        return x * wide(ct_ref[...]) + up * wide(s1_ref[...]) + down * wide(s2_ref[...])

    qn = _rms(cq_ref[...].astype(F32), qg_ref[...]).astype(BF16)
    q = rope(_dot(qn, wq_ref[...]), MLA_HEADS) * (1.0 / math.sqrt(MLA_NOPE + MLA_ROPE))
    q_out[...] = q.astype(BF16)

    kvn = _rms(ckv_ref[...].astype(F32), kvg_ref[...]).astype(BF16)
    kr = rope(kr_ref[...], 1)
    k = _dot(kvn, wk_ref[...]) + jnp.concatenate([kr] * MLA_HEADS, axis=1)
    k_out[...] = k.astype(BF16)
    lane = lax.broadcasted_iota(jnp.int32, (1, MLA_SLAB_W), 1) & (HEAD_SLAB - 1)
    v = _dot(kvn, wv_ref[...]) + (lane == V_ONES_LANE).astype(F32)
    v_out[...] = v.astype(BF16)


def _mlaproj(cq, ckv, kr, p, l, tabs):
    S = cq.shape[0]
    tm = TM_PROJ

    def rows(w_):
        return pl.BlockSpec((tm, w_), lambda i: (i, 0))

    def per_layer(*shape):
        return pl.BlockSpec((None,) + shape, lambda i: (l,) + (0,) * len(shape))

    out = jax.ShapeDtypeStruct((S, MLA_SLAB_W), BF16)
    return pl.pallas_call(
        _mlaproj_kernel,
        grid=(S // tm,),
        in_specs=[
            rows(MLA_Q_RANK), rows(MLA_KV_RANK), rows(LANES),
            per_layer(1, MLA_Q_RANK), per_layer(1, MLA_KV_RANK),
            per_layer(MLA_Q_RANK, MLA_SLAB_W), per_layer(MLA_KV_RANK, MLA_SLAB_W),
            per_layer(MLA_KV_RANK, MLA_SLAB_W),
            rows(LANES), rows(LANES), rows(LANES),
        ],
        out_specs=[rows(MLA_SLAB_W)] * 3,
        out_shape=[out, out, out],
        compiler_params=pltpu.CompilerParams(
            dimension_semantics=("arbitrary",), vmem_limit_bytes=VMEM_LIMIT_V7X),
        name="mlaproj",
    )(cq, ckv, kr, p["qg"], p["kvg"], p["wq"], p["wk"], p["wv"], *tabs)


def _flash_kernel(q_ref, k_ref, v_ref, o_ref):
    t = T_ATT
    qi = pl.program_id(1)
    heads = (0, 1)
    qs = [q_ref[:, hh * HEAD_SLAB:(hh + 1) * HEAD_SLAB] for hh in heads]
    row = lax.broadcasted_iota(jnp.int32, (t, t), 0)
    col = lax.broadcasted_iota(jnp.int32, (t, t), 1)

    def block(j, carry, masked):
        start = pl.multiple_of(j * t, t)
        new = []
        for hh in heads:
            m, acc = carry[2 * hh], carry[2 * hh + 1]
            k = k_ref[pl.ds(start, t), hh * HEAD_SLAB:(hh + 1) * HEAD_SLAB]
            v = v_ref[pl.ds(start, t), hh * HEAD_SLAB:(hh + 1) * HEAD_SLAB]
            s = _dot_nt(qs[hh], k)
            if masked:
                s = jnp.where(col <= row, s, -jnp.inf)
            m_new = jnp.maximum(m, jnp.max(s, axis=-1, keepdims=True))
            alpha = jnp.exp(m - m_new)
            p = jnp.exp(s - m_new)
            acc = alpha * acc + _dot(p.astype(BF16), v)
            new += [m_new, acc]
        return tuple(new)

    init = []
    for _ in heads:
        init += [jnp.full((t, 1), -jnp.inf, F32), jnp.zeros((t, HEAD_SLAB), F32)]
    carry = lax.fori_loop(0, qi, lambda j, c: block(j, c, False), tuple(init))
    carry = block(qi, carry, True)

    outs = []
    for hh in heads:
        acc = carry[2 * hh + 1]
        outs.append(acc * (1.0 / acc[:, V_ONES_LANE:V_ONES_LANE + 1]))
    lane = lax.broadcasted_iota(jnp.int32, (1, HEAD_SLAB), 1)
    o_ref[...] = jnp.where(lane < MLA_V, outs[0], pltpu.roll(outs[1], MLA_V, 1)).astype(BF16)


def _flash(q, k, v):
    S = q.shape[0]
    t = T_ATT
    pair_w = 2 * HEAD_SLAB
    return pl.pallas_call(
        _flash_kernel,
        grid=(MLA_HEADS // 2, S // t),
        in_specs=[
            pl.BlockSpec((t, pair_w), lambda pr, i: (i, pr)),
            pl.BlockSpec((S, pair_w), lambda pr, i: (0, pr)),
            pl.BlockSpec((S, pair_w), lambda pr, i: (0, pr)),
        ],
        out_specs=pl.BlockSpec((t, 2 * MLA_V), lambda pr, i: (i, pr)),
        out_shape=jax.ShapeDtypeStruct((S, MLA_W), BF16),
        compiler_params=pltpu.CompilerParams(
            dimension_semantics=("arbitrary", "arbitrary"), vmem_limit_bytes=VMEM_LIMIT_V7X),
        name="flash",
    )(q, k, v)


def _outmlp_kernel(x_ref, ya_ref, yb_ref, yc_ref, wo_ref, g_ref, w1_ref, w2_ref, fg_ref, o_ref,
                   hn_ref, *, final):
    c = pl.program_id(1)

    @pl.when(c == 0)
    def _():
        a0, a1, a2 = GMLP_W, GMLP_W + SSD_W, GMLP_W + SSD_W + MLA_W
        x = x_ref[...]
        x = x + _dot(ya_ref[...], wo_ref[0:a0, :])
        x = x + _dot(yb_ref[...], wo_ref[a0:a1, :])
        x = x + _dot(yc_ref[...], wo_ref[a1:a2, :])
        o_ref[...] = x
        hn_ref[...] = _rms(x, g_ref[...]).astype(BF16)

    h1 = _dot(hn_ref[...], w1_ref[...])
    h1 = jnp.square(jnp.maximum(h1, 0.0)).astype(BF16)
    o_ref[...] += _dot(h1, w2_ref[...])

    if final:
        @pl.when(c == pl.num_programs(1) - 1)
        def _():
            o_ref[...] = _rms(o_ref[...], fg_ref[...])


def _outmlp(x, ya, yb, yc, p, l, final_g, final):
    S = x.shape[0]
    tm = TM_MLP

    def rows(w_):
        return pl.BlockSpec((tm, w_), lambda i, c: (i, 0))

    def per_layer(*shape):
        return pl.BlockSpec((None,) + shape, lambda i, c: (l,) + (0,) * len(shape))

    return pl.pallas_call(
        functools.partial(_outmlp_kernel, final=final),
        grid=(S // tm, D_FF // FF_CHUNK),
        in_specs=[
            rows(D_MODEL), rows(GMLP_W), rows(SSD_W), rows(MLA_W),
            per_layer(D_MODEL, D_MODEL), per_layer(1, D_MODEL),
            pl.BlockSpec((None, D_MODEL, FF_CHUNK), lambda i, c: (l, 0, c)),
            pl.BlockSpec((None, FF_CHUNK, D_MODEL), lambda i, c: (l, c, 0)),
            pl.BlockSpec((1, D_MODEL), lambda i, c: (0, 0)),
        ],
        out_specs=rows(D_MODEL),
        out_shape=jax.ShapeDtypeStruct((S, D_MODEL), F32),
        scratch_shapes=[pltpu.VMEM((tm, D_MODEL), BF16)],
        compiler_params=pltpu.CompilerParams(
            dimension_semantics=("arbitrary", "arbitrary"), vmem_limit_bytes=VMEM_LIMIT_V7X),
        name="outmlp",
    )(x, ya, yb, yc, p["wo"], p["mlp_g"], p["w1"], p["w2"], final_g)


def _pad_lanes(a, n):
    return jnp.pad(a, [(0, 0)] * (a.ndim - 1) + [(0, n - a.shape[-1])])


def _head_slabs(w, heads, width, lane0):
    lead = w.shape[:-1]
    w = w.reshape(lead + (heads, width))
    w = jnp.pad(w, [(0, 0)] * len(lead) + [(0, 0), (lane0, HEAD_SLAB - lane0 - width)])
    return w.reshape(lead + (heads * HEAD_SLAB,))


def _layout_params(norm_mix_g, w_in, gmlp_v_norm_g, gmlp_w_s, gmlp_b_s, ssd_conv_w, ssd_conv_b,
                   ssd_dt_bias, ssd_a_log, ssd_d, ssd_norm_g, mla_q_norm_g, mla_w_qb,
                   mla_kv_norm_g, mla_w_kvb, w_out, norm_mlp_g, mlp_w1, mlp_w2):
    widths = (2 * GMLP_W, SSD_W, SSD_XBC, SSD_HEADS, MLA_Q_RANK, MLA_KV_RANK, MLA_ROPE)
    offs = np.cumsum((0,) + widths)
    uv, z, xbc, dt, cq, ckv, kr = [w_in[..., offs[i]:offs[i + 1]] for i in range(7)]
    kr = jnp.pad(kr, [(0, 0), (0, 0), (ROPE_LANE0, HEAD_SLAB - ROPE_LANE0 - MLA_ROPE)])
    w_in_p = jnp.concatenate([uv, z, xbc, cq, ckv, _pad_lanes(dt, LANES), kr], axis=-1).astype(BF16)

    qk_w = MLA_NOPE + MLA_ROPE
    wq = _head_slabs(mla_w_qb, MLA_HEADS, qk_w, 0)
    kv = mla_w_kvb.reshape(DEPTH, MLA_KV_RANK, MLA_HEADS, MLA_NOPE + MLA_V)
    wk = _head_slabs(kv[..., :MLA_NOPE].reshape(DEPTH, MLA_KV_RANK, -1), MLA_HEADS, MLA_NOPE, 0)
    wv = _head_slabs(kv[..., MLA_NOPE:].reshape(DEPTH, MLA_KV_RANK, -1), MLA_HEADS, MLA_V, 0)

    return {
        "mix_g": norm_mix_g[:, None, :],
        "w_in": w_in_p,
        "vg": gmlp_v_norm_g[:, None, :],
        "ws": gmlp_w_s,
        "bsf": jnp.repeat(jnp.swapaxes(gmlp_b_s, 1, 2), GMLP_HEAD_DIM, axis=-1),
        "cw": ssd_conv_w,
        "cb": ssd_conv_b[:, None, :],
        "dtb": _pad_lanes(ssd_dt_bias, LANES)[:, None, :],
        "alog": _pad_lanes(ssd_a_log, LANES)[:, None, :],
        "dsk": jnp.repeat(ssd_d, SSD_HEAD_DIM, axis=-1)[:, None, :],
        "ng": ssd_norm_g[:, None, :],
        "qg": mla_q_norm_g[:, None, :],
        "kvg": mla_kv_norm_g[:, None, :],
        "wq": wq.astype(BF16),
        "wk": wk.astype(BF16),
        "wv": wv.astype(BF16),
        "wo": w_out.astype(BF16),
        "mlp_g": norm_mlp_g[:, None, :],
        "w1": mlp_w1.astype(BF16),
        "w2": mlp_w2.astype(BF16),
    }


def _constant_tables():
    heads = np.arange(SSD_W) // SSD_HEAD_DIM
    expand_heads = (np.arange(LANES)[:, None] == heads[None, :]).astype(np.float32)
    groups = np.arange(SSD_W) // (SSD_W // SSD_GROUPS)
    group_expand = (np.arange(LANES)[:, None] == groups[None, :]).astype(np.float32)
    state_mask = ((np.arange(SSD_BC) // SSD_STATE)[:, None] == groups[None, :]).astype(np.float32)
    return {
        "expand_heads": jnp.asarray(expand_heads),
        "group_sum": jnp.asarray(group_expand.T.copy()),
        "group_expand": jnp.asarray(group_expand),
        "state_mask": jnp.asarray(state_mask),
    }


def _rope_tables(S):
    half = MLA_ROPE // 2
    pos = jnp.arange(S, dtype=F32)
    inv_freq = jnp.power(ROPE_BASE, -jnp.arange(half, dtype=F32) / half)
    ang = pos[:, None] * inv_freq[None, :]
    cos, sin = jnp.cos(ang), jnp.sin(ang)
    zeros = jnp.zeros((S, half), F32)
    pad = jnp.zeros((S, HEAD_SLAB - ROPE_LANE0 - MLA_ROPE), F32)
    ct = jnp.concatenate([jnp.ones((S, ROPE_LANE0), F32), cos, cos, pad], axis=1)
    s1 = jnp.concatenate([jnp.zeros((S, ROPE_LANE0), F32), -sin, zeros, pad], axis=1)
    s2 = jnp.concatenate([jnp.zeros((S, ROPE_LANE0), F32), zeros, sin, pad], axis=1)
    return ct, s1, s2


def kernel(x, norm_mix_g, w_in, gmlp_v_norm_g, gmlp_w_s, gmlp_b_s, ssd_conv_w, ssd_conv_b, ssd_dt_bias, ssd_a_log, ssd_d, ssd_norm_g, mla_q_norm_g, mla_w_qb, mla_kv_norm_g, mla_w_kvb, w_out, norm_mlp_g, mlp_w1, mlp_w2, final_norm_g):
    B, S, _ = x.shape
    assert B == 1 and S % T_ATT == 0 and S % TM_PROJ == 0 and S % TM_MLP == 0
    p = _layout_params(norm_mix_g, w_in, gmlp_v_norm_g, gmlp_w_s, gmlp_b_s, ssd_conv_w, ssd_conv_b,
                       ssd_dt_bias, ssd_a_log, ssd_d, ssd_norm_g, mla_q_norm_g, mla_w_qb,
                       mla_kv_norm_g, mla_w_kvb, w_out, norm_mlp_g, mlp_w1, mlp_w2)
    p.update(_constant_tables())
    tabs = _rope_tables(S)
    fg = final_norm_g[None, :]
    xs = x.reshape(S, D_MODEL)
    for l in range(DEPTH):
        uv, z, xbc, cq, ckv, dt, kr = _inproj(xs, p["mix_g"], p["w_in"], l)
        ya, yb = _mixab(uv, z, xbc, dt, p, l)
        q, k, v = _mlaproj(cq, ckv, kr, p, l, tabs)
        yc = _flash(q, k, v)
        xs = _outmlp(xs, ya, yb, yc, p, l, fg, final=(l == DEPTH - 1))
    return xs.reshape(B, S, D_MODEL)
```

```python
import functools
import math

import numpy as np
import jax
import jax.numpy as jnp
from jax import lax
from jax.experimental import pallas as pl
from jax.experimental.pallas import tpu as pltpu

F32 = jnp.float32
BF16 = jnp.bfloat16
HIGHEST = lax.Precision.HIGHEST

D_MODEL = 1024
DEPTH = 4
CHUNK = 128
GMLP_HEADS = 4
GMLP_HEAD_DIM = 64
GMLP_W = GMLP_HEADS * GMLP_HEAD_DIM
SSD_HEADS = 6
SSD_HEAD_DIM = 64
SSD_W = SSD_HEADS * SSD_HEAD_DIM
SSD_GROUPS = 2
SSD_STATE = 64
SSD_CONV = 4
SSD_BC = SSD_GROUPS * SSD_STATE
SSD_XBC = SSD_W + 2 * SSD_BC
MLA_HEADS = 6
MLA_NOPE = 64
MLA_ROPE = 32
MLA_V = 64
MLA_W = MLA_HEADS * MLA_V
MLA_Q_RANK = 384
MLA_KV_RANK = 256
ROPE_BASE = 10000.0
D_FF = 4 * D_MODEL
EPS = 1e-6

LANES = 128
HEAD_SLAB = LANES
MLA_SLAB_W = MLA_HEADS * HEAD_SLAB
V_ONES_ROW = MLA_V
ROPE_LANE0 = MLA_NOPE
VMEM_LIMIT_V7X = 56 * 1024 * 1024

_OFF_UV = 0
_OFF_Z = _OFF_UV + 2 * GMLP_W
_OFF_XBC = _OFF_Z + SSD_W
_OFF_CQ = _OFF_XBC + SSD_XBC
_OFF_CKV = _OFF_CQ + MLA_Q_RANK
_OFF_DT = _OFF_CKV + MLA_KV_RANK
_OFF_KR = _OFF_DT + LANES
D_IN_PAD = _OFF_KR + LANES

TM_PROJ = 512
TM_MLP = 1024
TQ_ATT = 512
TK_ATT = TQ_ATT // 2
FF_CHUNK = 1024


def _rms(x, g):
    var = jnp.mean(x * x, axis=-1, keepdims=True)
    return (x * lax.rsqrt(var + EPS)) * g


def _sigmoid(x):
    return 1.0 / (1.0 + jnp.exp(-x))


def _lane_group(width, group_width):
    shift = group_width.bit_length() - 1
    assert 1 << shift == group_width
    return lax.shift_right_logical(lax.broadcasted_iota(jnp.int32, (1, width), 1), shift)


def _dot(a, b, precision=None):
    return jnp.dot(a, b, preferred_element_type=F32, precision=precision)


def _dot_nt(a, b):
    return lax.dot_general(a, b, (((1,), (1,)), ((), ())), preferred_element_type=F32)


def _inproj_kernel(x_ref, g_ref, w_ref, uv_ref, z_ref, xbc_ref, cq_ref, ckv_ref, dt_ref, kr_ref):
    h = _rms(x_ref[...], g_ref[...]).astype(BF16)

    def seg(a, b):
        return _dot(h, w_ref[:, a:b])

    uv_ref[...] = seg(_OFF_UV, _OFF_Z).astype(BF16)
    z_ref[...] = seg(_OFF_Z, _OFF_XBC).astype(BF16)
    xbc_ref[...] = seg(_OFF_XBC, _OFF_CQ).astype(BF16)
    cq_ref[...] = seg(_OFF_CQ, _OFF_CKV).astype(BF16)
    ckv_ref[...] = seg(_OFF_CKV, _OFF_DT).astype(BF16)
    dt_ref[...] = seg(_OFF_DT, _OFF_KR)
    kr_ref[...] = seg(_OFF_KR, D_IN_PAD)


def _inproj(x, g, w, l):
    S = x.shape[0]
    tm = TM_PROJ
    widths = (2 * GMLP_W, SSD_W, SSD_XBC, MLA_Q_RANK, MLA_KV_RANK, LANES, LANES)
    dtypes = (BF16, BF16, BF16, BF16, BF16, F32, F32)
    return pl.pallas_call(
        _inproj_kernel,
        grid=(S // tm,),
        in_specs=[
            pl.BlockSpec((tm, D_MODEL), lambda i: (i, 0)),
            pl.BlockSpec((None, 1, D_MODEL), lambda i: (l, 0, 0)),
            pl.BlockSpec((None, D_MODEL, D_IN_PAD), lambda i: (l, 0, 0)),
        ],
        out_specs=[pl.BlockSpec((tm, w_), lambda i: (i, 0)) for w_ in widths],
        out_shape=[jax.ShapeDtypeStruct((S, w_), d_) for w_, d_ in zip(widths, dtypes)],
        compiler_params=pltpu.CompilerParams(
            dimension_semantics=("arbitrary",), vmem_limit_bytes=VMEM_LIMIT_V7X),
        name="inproj",
    )(x, g, w)


def _mixab_kernel(uv_ref, z_ref, xbc_ref, dt_ref, vg_ref, ws_ref, bsf_ref, cw_ref, cb_ref,
                  dtb_ref, alog_ref, dsk_ref, ng_ref, e_ref, g_ref, gt_ref, bd_ref,
                  ya_ref, yb_ref, state_ref, tail_ref):
    L = CHUNK

    @pl.when(pl.program_id(0) == 0)
    def _():
        state_ref[...] = jnp.zeros_like(state_ref)
        tail_ref[...] = jnp.zeros_like(tail_ref)

    row = lax.broadcasted_iota(jnp.int32, (L, L), 0)
    col = lax.broadcasted_iota(jnp.int32, (L, L), 1)
    causal = col <= row
    tril = causal.astype(F32)

    uv = uv_ref[...].astype(F32)
    guv = 0.5 * uv * (1.0 + jnp.tanh(math.sqrt(2.0 / math.pi) * (uv + 0.044715 * (uv * uv * uv))))
    u = guv[:, :GMLP_W]
    vn = _rms(guv[:, GMLP_W:], vg_ref[...])
    head_a = _lane_group(GMLP_W, GMLP_HEAD_DIM)
    mixed = bsf_ref[...]
    for h in range(GMLP_HEADS):
        wm = (ws_ref[h] * tril).astype(BF16)
        vm = (vn * (head_a == h).astype(F32)).astype(BF16)
        mixed = mixed + _dot(wm, vm)
    ya_ref[...] = (u * mixed).astype(BF16)

    x = xbc_ref[...].astype(F32)
    xfull = jnp.concatenate([tail_ref[...], x], axis=0)
    conv = cb_ref[...] + cw_ref[SSD_CONV - 1:SSD_CONV, :] * x
    for k in range(SSD_CONV - 1):
        shifted = pltpu.roll(xfull, SSD_CONV - 1 - k, 0)[8:8 + L]
        conv = conv + cw_ref[k:k + 1, :] * shifted
    tail_ref[...] = x[L - 8:L]
    xc = conv * _sigmoid(conv)
    xs = xc[:, :SSD_W]
    Bm = xc[:, SSD_W:SSD_W + SSD_BC]
    Cm = xc[:, SSD_W + SSD_BC:]

    dtr = dt_ref[...] + dtb_ref[...]
    dt = jnp.maximum(dtr, 0.0) + jnp.log1p(jnp.exp(-jnp.abs(dtr)))
    a_neg = -jnp.exp(alog_ref[...])
    a_col = _dot(tril, dt * a_neg, HIGHEST)
    a_row = a_col.T
    dt_row = dt.T

    Bb = Bm.astype(BF16)
    Cb = Cm.astype(BF16)
    group_bc = _lane_group(SSD_BC, SSD_STATE)
    head_b = _lane_group(SSD_W, SSD_HEAD_DIM)
    hpg = SSD_HEADS // SSD_GROUPS
    y = xs * dsk_ref[...]
    for g in range(SSD_GROUPS):
        cg = (Cm * (group_bc == g).astype(F32)).astype(BF16)
        cbg = _dot_nt(cg, Bb)
        for hh in range(hpg):
            h = g * hpg + hh
            seg = a_col[:, h:h + 1] - a_row[h:h + 1, :]
            decay = jnp.exp(jnp.where(causal, seg, -jnp.inf))
            w = (cbg * decay * dt_row[h:h + 1, :]).astype(BF16)
            xm = (xs * (head_b == h).astype(F32)).astype(BF16)
            y = y + _dot(w, xm)

    ea = jnp.exp(a_col)
    dte = jnp.exp(a_col[L - 1:L, :] - a_col) * dt
    ea_w = _dot(ea, e_ref[...], HIGHEST)
    dte_w = _dot(dte, e_ref[...], HIGHEST)
    s_prev = state_ref[...]
    y = y + _dot(Cb, s_prev.astype(BF16)) * ea_w
    xsd = (xs * dte_w).astype(BF16)
    bt = Bm.T.astype(BF16)
    state_ref[...] = ea_w[L - 1:L, :] * s_prev + bd_ref[...] * _dot(bt, xsd)

    zf = z_ref[...].astype(F32)
    yg = y * (zf * _sigmoid(zf))
    ssq = _dot(yg * yg, g_ref[...], HIGHEST)
    inv = lax.rsqrt(ssq * (1.0 / (SSD_W // SSD_GROUPS)) + EPS)
    yb_ref[...] = (yg * _dot(inv, gt_ref[...], HIGHEST) * ng_ref[...]).astype(BF16)


def _mixab(uv, z, xbc, dt, p, l):
    S = uv.shape[0]
    L = CHUNK

    def rows(w_):
        return pl.BlockSpec((L, w_), lambda i: (i, 0))

    def per_layer(*shape):
        return pl.BlockSpec((None,) + shape, lambda i: (l,) + (0,) * len(shape))

    def const(*shape):
        return pl.BlockSpec(shape, lambda i: (0,) * len(shape))

    return pl.pallas_call(
        _mixab_kernel,
        grid=(S // L,),
        in_specs=[
            rows(2 * GMLP_W), rows(SSD_W), rows(SSD_XBC), rows(LANES),
            per_layer(1, GMLP_W), per_layer(GMLP_HEADS, L, L), per_layer(L, GMLP_W),
            per_layer(SSD_CONV, SSD_XBC), per_layer(1, SSD_XBC),
            per_layer(1, LANES), per_layer(1, LANES), per_layer(1, SSD_W), per_layer(1, SSD_W),
            const(LANES, SSD_W), const(SSD_W, LANES), const(LANES, SSD_W), const(SSD_BC, SSD_W),
        ],
        out_specs=[rows(GMLP_W), rows(SSD_W)],
        out_shape=[jax.ShapeDtypeStruct((S, GMLP_W), BF16), jax.ShapeDtypeStruct((S, SSD_W), BF16)],
        scratch_shapes=[pltpu.VMEM((SSD_BC, SSD_W), F32), pltpu.VMEM((8, SSD_XBC), F32)],
        compiler_params=pltpu.CompilerParams(
            dimension_semantics=("arbitrary",), vmem_limit_bytes=VMEM_LIMIT_V7X),
        name="mixab",
    )(uv, z, xbc, dt, p["vg"], p["ws"], p["bsf"], p["cw"], p["cb"], p["dtb"], p["alog"],
      p["dsk"], p["ng"], p["expand_heads"], p["group_sum"], p["group_expand"], p["state_mask"])


def _mlaproj_kernel(cq_ref, ckv_ref, kr_ref, qg_ref, kvg_ref, wq_ref, wk_ref, wvt_ref,
                    ct_ref, s1_ref, s2_ref, q_out, k_out, vt_out):
    def rope(x, n):
        def wide(t):
            return t if n == 1 else jnp.concatenate([t] * n, axis=1)
        w = n * HEAD_SLAB
        up = pltpu.roll(x, w - MLA_ROPE // 2, 1)
        down = pltpu.roll(x, MLA_ROPE // 2, 1)
        return x * wide(ct_ref[...]) + up * wide(s1_ref[...]) + down * wide(s2_ref[...])

    qn = _rms(cq_ref[...].astype(F32), qg_ref[...]).astype(BF16)
    q = rope(_dot(qn, wq_ref[...]), MLA_HEADS) * (1.0 / math.sqrt(MLA_NOPE + MLA_ROPE))
    q_out[...] = q.astype(BF16)

    kvn = _rms(ckv_ref[...].astype(F32), kvg_ref[...]).astype(BF16)
    kr = rope(kr_ref[...], 1)
    k = _dot(kvn, wk_ref[...]) + jnp.concatenate([kr] * MLA_HEADS, axis=1)
    k_out[...] = k.astype(BF16)
    srow = lax.broadcasted_iota(jnp.int32, (MLA_SLAB_W, 1), 0) & (HEAD_SLAB - 1)
    vt = _dot_nt(wvt_ref[...], kvn) + (srow == V_ONES_ROW).astype(F32)
    vt = vt.astype(BF16)
    for b in range(TM_PROJ // TK_ATT):
        vt_out[b] = vt[:, b * TK_ATT:(b + 1) * TK_ATT]


def _mlaproj(cq, ckv, kr, p, l, tabs):
    S = cq.shape[0]
    tm = TM_PROJ

    def rows(w_):
        return pl.BlockSpec((tm, w_), lambda i: (i, 0))

    def per_layer(*shape):
        return pl.BlockSpec((None,) + shape, lambda i: (l,) + (0,) * len(shape))

    out = jax.ShapeDtypeStruct((S, MLA_SLAB_W), BF16)
    nb = tm // TK_ATT
    return pl.pallas_call(
        _mlaproj_kernel,
        grid=(S // tm,),
        in_specs=[
            rows(MLA_Q_RANK), rows(MLA_KV_RANK), rows(LANES),
            per_layer(1, MLA_Q_RANK), per_layer(1, MLA_KV_RANK),
            per_layer(MLA_Q_RANK, MLA_SLAB_W), per_layer(MLA_KV_RANK, MLA_SLAB_W),
            per_layer(MLA_SLAB_W, MLA_KV_RANK),
            rows(LANES), rows(LANES), rows(LANES),
        ],
        out_specs=[rows(MLA_SLAB_W), rows(MLA_SLAB_W),
                   pl.BlockSpec((nb, MLA_SLAB_W, TK_ATT), lambda i: (i, 0, 0))],
        out_shape=[out, out, jax.ShapeDtypeStruct((S // TK_ATT, MLA_SLAB_W, TK_ATT), BF16)],
        compiler_params=pltpu.CompilerParams(
            dimension_semantics=("arbitrary",), vmem_limit_bytes=VMEM_LIMIT_V7X),
        name="mlaproj",
    )(cq, ckv, kr, p["qg"], p["kvg"], p["wq"], p["wk"], p["wvt"], *tabs)


def _flash_kernel(q_ref, k_ref, vt_ref, o_ref, sa_ref, sb_ref, m_ref, acc_ref):
    tq, tk = TQ_ATT, TK_ATT
    qi = pl.program_id(1)
    heads = (0, 1)

    def slab(hh):
        return slice(hh * HEAD_SLAB, (hh + 1) * HEAD_SLAB)

    def scores(blk, s_ref):
        start = pl.multiple_of(blk * tk, tk)
        for hh in heads:
            s_ref[hh] = _dot_nt(k_ref[pl.ds(start, tk), slab(hh)], q_ref[:, slab(hh)])

    def softmax_pv(blk, s_ref, mask):
        for hh in heads:
            s = s_ref[hh]
            if mask is not None:
                s = jnp.where(mask, s, -jnp.inf)
            m_old = m_ref[hh]
            m_new = jnp.maximum(m_old, jnp.max(s, axis=0, keepdims=True))
            alpha = jnp.exp(m_old - m_new)
            p = jnp.exp(s - m_new).astype(BF16)
            acc_ref[hh] = alpha * acc_ref[hh] + _dot(vt_ref[blk, slab(hh), :], p)
            m_ref[hh] = m_new

    m_ref[...] = jnp.full_like(m_ref, -jnp.inf)
    acc_ref[...] = jnp.zeros_like(acc_ref)
    scores(0, sa_ref)

    def body(j, carry):
        scores(2 * j + 1, sb_ref)
        softmax_pv(2 * j, sa_ref, None)
        scores(2 * j + 2, sa_ref)
        softmax_pv(2 * j + 1, sb_ref, None)
        return carry

    lax.fori_loop(0, qi, body, 0)

    key = lax.broadcasted_iota(jnp.int32, (tk, tq), 0)
    qry = lax.broadcasted_iota(jnp.int32, (tk, tq), 1)
    scores(2 * qi + 1, sb_ref)
    softmax_pv(2 * qi, sa_ref, key <= qry)
    softmax_pv(2 * qi + 1, sb_ref, key + tk <= qry)

    outs = []
    for hh in heads:
        acc = acc_ref[hh]
        outs.append((acc * (1.0 / acc[V_ONES_ROW:V_ONES_ROW + 1, :])).T)
    lane = lax.broadcasted_iota(jnp.int32, (1, HEAD_SLAB), 1)
    o_ref[...] = jnp.where(lane < MLA_V, outs[0], pltpu.roll(outs[1], MLA_V, 1)).astype(BF16)


def _flash(q, k, vt):
    S = q.shape[0]
    tq, tk = TQ_ATT, TK_ATT
    pair_w = 2 * HEAD_SLAB
    return pl.pallas_call(
        _flash_kernel,
        grid=(MLA_HEADS // 2, S // tq),
        in_specs=[
            pl.BlockSpec((tq, pair_w), lambda pr, i: (i, pr)),
            pl.BlockSpec((S, pair_w), lambda pr, i: (0, pr)),
            pl.BlockSpec((S // tk, pair_w, tk), lambda pr, i: (0, pr, 0)),
        ],
        out_specs=pl.BlockSpec((tq, 2 * MLA_V), lambda pr, i: (i, pr)),
        out_shape=jax.ShapeDtypeStruct((S, MLA_W), BF16),
        scratch_shapes=[
            pltpu.VMEM((2, tk, tq), F32), pltpu.VMEM((2, tk, tq), F32),
            pltpu.VMEM((2, 1, tq), F32), pltpu.VMEM((2, HEAD_SLAB, tq), F32),
        ],
        compiler_params=pltpu.CompilerParams(
            dimension_semantics=("arbitrary", "arbitrary"), vmem_limit_bytes=VMEM_LIMIT_V7X),
        name="flash",
    )(q, k, vt)


def _outmlp_kernel(x_ref, ya_ref, yb_ref, yc_ref, wo_ref, g_ref, w1_ref, w2_ref, fg_ref, o_ref,
                   hn_ref, *, final):
    c = pl.program_id(1)

    @pl.when(c == 0)
    def _():
        a0, a1, a2 = GMLP_W, GMLP_W + SSD_W, GMLP_W + SSD_W + MLA_W
        x = x_ref[...]
        x = x + _dot(ya_ref[...], wo_ref[0:a0, :])
        x = x + _dot(yb_ref[...], wo_ref[a0:a1, :])
        x = x + _dot(yc_ref[...], wo_ref[a1:a2, :])
        o_ref[...] = x
        hn_ref[...] = _rms(x, g_ref[...]).astype(BF16)

    h1 = _dot(hn_ref[...], w1_ref[...])
    h1 = jnp.square(jnp.maximum(h1, 0.0)).astype(BF16)
    o_ref[...] += _dot(h1, w2_ref[...])

    if final:
        @pl.when(c == pl.num_programs(1) - 1)
        def _():
            o_ref[...] = _rms(o_ref[...], fg_ref[...])


def _outmlp(x, ya, yb, yc, p, l, final_g, final):
    S = x.shape[0]
    tm = TM_MLP

    def rows(w_):
        return pl.BlockSpec((tm, w_), lambda i, c: (i, 0))

    def per_layer(*shape):
        return pl.BlockSpec((None,) + shape, lambda i, c: (l,) + (0,) * len(shape))

    return pl.pallas_call(
        functools.partial(_outmlp_kernel, final=final),
        grid=(S // tm, D_FF // FF_CHUNK),
        in_specs=[
            rows(D_MODEL), rows(GMLP_W), rows(SSD_W), rows(MLA_W),
            per_layer(D_MODEL, D_MODEL), per_layer(1, D_MODEL),
            pl.BlockSpec((None, D_MODEL, FF_CHUNK), lambda i, c: (l, 0, c)),
            pl.BlockSpec((None, FF_CHUNK, D_MODEL), lambda i, c: (l, c, 0)),
            pl.BlockSpec((1, D_MODEL), lambda i, c: (0, 0)),
        ],
        out_specs=rows(D_MODEL),
        out_shape=jax.ShapeDtypeStruct((S, D_MODEL), F32),
        scratch_shapes=[pltpu.VMEM((tm, D_MODEL), BF16)],
        compiler_params=pltpu.CompilerParams(
            dimension_semantics=("arbitrary", "arbitrary"), vmem_limit_bytes=VMEM_LIMIT_V7X),
        name="outmlp",
    )(x, ya, yb, yc, p["wo"], p["mlp_g"], p["w1"], p["w2"], final_g)


def _pad_lanes(a, n):
    return jnp.pad(a, [(0, 0)] * (a.ndim - 1) + [(0, n - a.shape[-1])])


def _head_slabs(w, heads, width, lane0):
    lead = w.shape[:-1]
    w = w.reshape(lead + (heads, width))
    w = jnp.pad(w, [(0, 0)] * len(lead) + [(0, 0), (lane0, HEAD_SLAB - lane0 - width)])
    return w.reshape(lead + (heads * HEAD_SLAB,))


def _layout_params(norm_mix_g, w_in, gmlp_v_norm_g, gmlp_w_s, gmlp_b_s, ssd_conv_w, ssd_conv_b,
                   ssd_dt_bias, ssd_a_log, ssd_d, ssd_norm_g, mla_q_norm_g, mla_w_qb,
                   mla_kv_norm_g, mla_w_kvb, w_out, norm_mlp_g, mlp_w1, mlp_w2):
    widths = (2 * GMLP_W, SSD_W, SSD_XBC, SSD_HEADS, MLA_Q_RANK, MLA_KV_RANK, MLA_ROPE)
    offs = np.cumsum((0,) + widths)
    uv, z, xbc, dt, cq, ckv, kr = [w_in[..., offs[i]:offs[i + 1]] for i in range(7)]
    kr = jnp.pad(kr, [(0, 0), (0, 0), (ROPE_LANE0, HEAD_SLAB - ROPE_LANE0 - MLA_ROPE)])
    w_in_p = jnp.concatenate([uv, z, xbc, cq, ckv, _pad_lanes(dt, LANES), kr], axis=-1).astype(BF16)

    qk_w = MLA_NOPE + MLA_ROPE
    wq = _head_slabs(mla_w_qb, MLA_HEADS, qk_w, 0)
    kv = mla_w_kvb.reshape(DEPTH, MLA_KV_RANK, MLA_HEADS, MLA_NOPE + MLA_V)
    wk = _head_slabs(kv[..., :MLA_NOPE].reshape(DEPTH, MLA_KV_RANK, -1), MLA_HEADS, MLA_NOPE, 0)
    wv = _head_slabs(kv[..., MLA_NOPE:].reshape(DEPTH, MLA_KV_RANK, -1), MLA_HEADS, MLA_V, 0)

    return {
        "mix_g": norm_mix_g[:, None, :],
        "w_in": w_in_p,
        "vg": gmlp_v_norm_g[:, None, :],
        "ws": gmlp_w_s,
        "bsf": jnp.repeat(jnp.swapaxes(gmlp_b_s, 1, 2), GMLP_HEAD_DIM, axis=-1),
        "cw": ssd_conv_w,
        "cb": ssd_conv_b[:, None, :],
        "dtb": _pad_lanes(ssd_dt_bias, LANES)[:, None, :],
        "alog": _pad_lanes(ssd_a_log, LANES)[:, None, :],
        "dsk": jnp.repeat(ssd_d, SSD_HEAD_DIM, axis=-1)[:, None, :],
        "ng": ssd_norm_g[:, None, :],
        "qg": mla_q_norm_g[:, None, :],
        "kvg": mla_kv_norm_g[:, None, :],
        "wq": wq.astype(BF16),
        "wk": wk.astype(BF16),
        "wvt": jnp.swapaxes(wv, 1, 2).astype(BF16),
        "wo": w_out.astype(BF16),
        "mlp_g": norm_mlp_g[:, None, :],
        "w1": mlp_w1.astype(BF16),
        "w2": mlp_w2.astype(BF16),
    }


def _constant_tables():
    heads = np.arange(SSD_W) // SSD_HEAD_DIM
    expand_heads = (np.arange(LANES)[:, None] == heads[None, :]).astype(np.float32)
    groups = np.arange(SSD_W) // (SSD_W // SSD_GROUPS)
    group_expand = (np.arange(LANES)[:, None] == groups[None, :]).astype(np.float32)
    state_mask = ((np.arange(SSD_BC) // SSD_STATE)[:, None] == groups[None, :]).astype(np.float32)
    return {
        "expand_heads": jnp.asarray(expand_heads),
        "group_sum": jnp.asarray(group_expand.T.copy()),
        "group_expand": jnp.asarray(group_expand),
        "state_mask": jnp.asarray(state_mask),
    }


def _rope_tables(S):
    half = MLA_ROPE // 2
    pos = jnp.arange(S, dtype=F32)
    inv_freq = jnp.power(ROPE_BASE, -jnp.arange(half, dtype=F32) / half)
    ang = pos[:, None] * inv_freq[None, :]
    cos, sin = jnp.cos(ang), jnp.sin(ang)
    zeros = jnp.zeros((S, half), F32)
    pad = jnp.zeros((S, HEAD_SLAB - ROPE_LANE0 - MLA_ROPE), F32)
    ct = jnp.concatenate([jnp.ones((S, ROPE_LANE0), F32), cos, cos, pad], axis=1)
    s1 = jnp.concatenate([jnp.zeros((S, ROPE_LANE0), F32), -sin, zeros, pad], axis=1)
    s2 = jnp.concatenate([jnp.zeros((S, ROPE_LANE0), F32), zeros, sin, pad], axis=1)
    return ct, s1, s2


def kernel(x, norm_mix_g, w_in, gmlp_v_norm_g, gmlp_w_s, gmlp_b_s, ssd_conv_w, ssd_conv_b, ssd_dt_bias, ssd_a_log, ssd_d, ssd_norm_g, mla_q_norm_g, mla_w_qb, mla_kv_norm_g, mla_w_kvb, w_out, norm_mlp_g, mlp_w1, mlp_w2, final_norm_g):
    B, S, _ = x.shape
    assert B == 1 and S % TQ_ATT == 0 and S % TM_PROJ == 0 and S % TM_MLP == 0
    p = _layout_params(norm_mix_g, w_in, gmlp_v_norm_g, gmlp_w_s, gmlp_b_s, ssd_conv_w, ssd_conv_b,
                       ssd_dt_bias, ssd_a_log, ssd_d, ssd_norm_g, mla_q_norm_g, mla_w_qb,
                       mla_kv_norm_g, mla_w_kvb, w_out, norm_mlp_g, mlp_w1, mlp_w2)
    p.update(_constant_tables())
    tabs = _rope_tables(S)
    fg = final_norm_g[None, :]
    xs = x.reshape(S, D_MODEL)
    for l in range(DEPTH):
        uv, z, xbc, cq, ckv, dt, kr = _inproj(xs, p["mix_g"], p["w_in"], l)
        ya, yb = _mixab(uv, z, xbc, dt, p, l)
        q, k, vt = _mlaproj(cq, ckv, kr, p, l, tabs)
        yc = _flash(q, k, vt)
        xs = _outmlp(xs, ya, yb, yc, p, l, fg, final=(l == DEPTH - 1))
    return xs.reshape(B, S, D_MODEL)
```

```python
import functools
import math

import numpy as np
import jax
import jax.numpy as jnp
from jax import lax
from jax.experimental import pallas as pl
from jax.experimental.pallas import tpu as pltpu

F32 = jnp.float32
BF16 = jnp.bfloat16
HIGHEST = lax.Precision.HIGHEST

D_MODEL = 1024
DEPTH = 4
CHUNK = 128
GMLP_HEADS = 4
GMLP_HEAD_DIM = 64
GMLP_W = GMLP_HEADS * GMLP_HEAD_DIM
SSD_HEADS = 6
SSD_HEAD_DIM = 64
SSD_W = SSD_HEADS * SSD_HEAD_DIM
SSD_GROUPS = 2
SSD_STATE = 64
SSD_CONV = 4
SSD_BC = SSD_GROUPS * SSD_STATE
SSD_XBC = SSD_W + 2 * SSD_BC
MLA_HEADS = 6
MLA_NOPE = 64
MLA_ROPE = 32
MLA_V = 64
MLA_W = MLA_HEADS * MLA_V
MLA_Q_RANK = 384
MLA_KV_RANK = 256
ROPE_BASE = 10000.0
D_FF = 4 * D_MODEL
EPS = 1e-6

LANES = 128
HEAD_SLAB = LANES
MLA_SLAB_W = MLA_HEADS * HEAD_SLAB
V_ONES_ROW = MLA_V
ROPE_LANE0 = MLA_NOPE
VMEM_LIMIT_V7X = 56 * 1024 * 1024

_OFF_UV = 0
_OFF_Z = _OFF_UV + 2 * GMLP_W
_OFF_XBC = _OFF_Z + SSD_W
_OFF_CQ = _OFF_XBC + SSD_XBC
_OFF_CKV = _OFF_CQ + MLA_Q_RANK
_OFF_DT = _OFF_CKV + MLA_KV_RANK
_OFF_KR = _OFF_DT + LANES
D_IN_PAD = _OFF_KR + LANES

TM_PROJ = 512
TM_MLP = 1024
TQ_ATT = 1024
TK_ATT = 256
FF_CHUNK = 1024


def _rms(x, g):
    var = jnp.mean(x * x, axis=-1, keepdims=True)
    return (x * lax.rsqrt(var + EPS)) * g


def _sigmoid(x):
    return 1.0 / (1.0 + jnp.exp(-x))


def _lane_group(width, group_width):
    shift = group_width.bit_length() - 1
    assert 1 << shift == group_width
    return lax.shift_right_logical(lax.broadcasted_iota(jnp.int32, (1, width), 1), shift)


def _dot(a, b, precision=None):
    return jnp.dot(a, b, preferred_element_type=F32, precision=precision)


def _dot_nt(a, b):
    return lax.dot_general(a, b, (((1,), (1,)), ((), ())), preferred_element_type=F32)


def _inproj_kernel(x_ref, g_ref, w_ref, uv_ref, z_ref, xbc_ref, cq_ref, ckv_ref, dt_ref, kr_ref):
    h = _rms(x_ref[...], g_ref[...]).astype(BF16)

    def seg(a, b):
        return _dot(h, w_ref[:, a:b])

    uv_ref[...] = seg(_OFF_UV, _OFF_Z).astype(BF16)
    z_ref[...] = seg(_OFF_Z, _OFF_XBC).astype(BF16)
    xbc_ref[...] = seg(_OFF_XBC, _OFF_CQ).astype(BF16)
    cq_ref[...] = seg(_OFF_CQ, _OFF_CKV).astype(BF16)
    ckv_ref[...] = seg(_OFF_CKV, _OFF_DT).astype(BF16)
    dt_ref[...] = seg(_OFF_DT, _OFF_KR)
    kr_ref[...] = seg(_OFF_KR, D_IN_PAD)


def _inproj(x, g, w, l):
    S = x.shape[0]
    tm = TM_PROJ
    widths = (2 * GMLP_W, SSD_W, SSD_XBC, MLA_Q_RANK, MLA_KV_RANK, LANES, LANES)
    dtypes = (BF16, BF16, BF16, BF16, BF16, F32, F32)
    return pl.pallas_call(
        _inproj_kernel,
        grid=(S // tm,),
        in_specs=[
            pl.BlockSpec((tm, D_MODEL), lambda i: (i, 0)),
            pl.BlockSpec((None, 1, D_MODEL), lambda i: (l, 0, 0)),
            pl.BlockSpec((None, D_MODEL, D_IN_PAD), lambda i: (l, 0, 0)),
        ],
        out_specs=[pl.BlockSpec((tm, w_), lambda i: (i, 0)) for w_ in widths],
        out_shape=[jax.ShapeDtypeStruct((S, w_), d_) for w_, d_ in zip(widths, dtypes)],
        compiler_params=pltpu.CompilerParams(
            dimension_semantics=("arbitrary",), vmem_limit_bytes=VMEM_LIMIT_V7X),
        name="inproj",
    )(x, g, w)


def _mixab_kernel(uv_ref, z_ref, xbc_ref, dt_ref, vg_ref, ws_ref, bsf_ref, cw_ref, cb_ref,
                  dtb_ref, alog_ref, dsk_ref, ng_ref, e2_ref, bd_ref,
                  ya_ref, yb_ref, state_ref, tail_ref, wm_ref):
    L = CHUNK
    row = lax.broadcasted_iota(jnp.int32, (L, L), 0)
    col = lax.broadcasted_iota(jnp.int32, (L, L), 1)
    causal = col <= row

    @pl.when(pl.program_id(0) == 0)
    def _():
        state_ref[...] = jnp.zeros_like(state_ref)
        tail_ref[...] = jnp.zeros_like(tail_ref)
        for h in range(GMLP_HEADS):
            wm_ref[h] = jnp.where(causal, ws_ref[h], 0.0).astype(BF16)

    uv = uv_ref[...].astype(F32)
    guv = 0.5 * uv * (1.0 + jnp.tanh(math.sqrt(2.0 / math.pi) * (uv + 0.044715 * (uv * uv * uv))))
    u = guv[:, :GMLP_W]
    vn = _rms(guv[:, GMLP_W:], vg_ref[...])
    head_a = _lane_group(GMLP_W, GMLP_HEAD_DIM)
    mixed = bsf_ref[...]
    for h in range(GMLP_HEADS):
        vm = jnp.where(head_a == h, vn, 0.0).astype(BF16)
        mixed = mixed + _dot(wm_ref[h], vm)
    ya_ref[...] = (u * mixed).astype(BF16)

    x = xbc_ref[...].astype(F32)
    xfull = jnp.concatenate([tail_ref[...], x], axis=0)
    conv = cb_ref[...] + cw_ref[SSD_CONV - 1:SSD_CONV, :] * x
    for k in range(SSD_CONV - 1):
        shifted = pltpu.roll(xfull, SSD_CONV - 1 - k, 0)[8:8 + L]
        conv = conv + cw_ref[k:k + 1, :] * shifted
    tail_ref[...] = x[L - 8:L]
    xc = conv * _sigmoid(conv)
    xs = xc[:, :SSD_W]
    Bm = xc[:, SSD_W:SSD_W + SSD_BC]
    Cm = xc[:, SSD_W + SSD_BC:]

    dtr = dt_ref[...] + dtb_ref[...]
    dt = jnp.maximum(dtr, 0.0) + jnp.log1p(jnp.exp(-jnp.abs(dtr)))
    a_col = dt * -jnp.exp(alog_ref[...])
    for k in range(CHUNK.bit_length() - 1):
        a_col = a_col + jnp.where(row >= (1 << k), pltpu.roll(a_col, 1 << k, 0), 0.0)
    a_row = a_col.T
    dt_row = dt.T

    Bb = Bm.astype(BF16)
    Cb = Cm.astype(BF16)
    hpg = SSD_HEADS // SSD_GROUPS
    group_bc = _lane_group(SSD_BC, SSD_STATE)
    cb = [_dot_nt(jnp.where(group_bc == g, Cm, 0.0).astype(BF16), Bb) for g in range(SSD_GROUPS)]
    first_half = lax.broadcasted_iota(jnp.int32, (1, LANES), 1) < SSD_HEAD_DIM
    slabs = []
    for j in range(SSD_W // LANES):
        xsl = xs[:, j * LANES:(j + 1) * LANES]
        acc = None
        for hh in range(LANES // SSD_HEAD_DIM):
            h = j * (LANES // SSD_HEAD_DIM) + hh
            seg = a_col[:, h:h + 1] - a_row[h:h + 1, :]
            decay = jnp.exp(jnp.where(causal, seg, -jnp.inf))
            w = (cb[h // hpg] * decay * dt_row[h:h + 1, :]).astype(BF16)
            xm = jnp.where(first_half == (hh == 0), xsl, 0.0).astype(BF16)
            term = _dot(w, xm)
            acc = term if acc is None else acc + term
        slabs.append(acc)
    y = xs * dsk_ref[...] + jnp.concatenate(slabs, axis=1)

    def expand_heads(v):
        hi = v.astype(BF16)
        lo = (v - hi.astype(F32)).astype(BF16)
        return _dot(jnp.concatenate([hi, lo], axis=1), e2_ref[...])

    ea_w = expand_heads(jnp.exp(a_col))
    dte_w = expand_heads(jnp.exp(a_col[L - 1:L, :] - a_col) * dt)
    s_prev = state_ref[...]
    y = y + _dot(Cb, s_prev.astype(BF16)) * ea_w
    xsd = (xs * dte_w).astype(BF16)
    bt = Bm.T.astype(BF16)
    state_ref[...] = ea_w[L - 1:L, :] * s_prev + bd_ref[...] * _dot(bt, xsd)

    zf = z_ref[...].astype(F32)
    yg = y * (zf * _sigmoid(zf))
    yg2 = yg * yg
    group0 = lax.broadcasted_iota(jnp.int32, (1, SSD_W), 1) < SSD_W // SSD_GROUPS
    inv = [lax.rsqrt(jnp.sum(jnp.where(group0 == (g == 0), yg2, 0.0), axis=-1, keepdims=True)
                     * (1.0 / (SSD_W // SSD_GROUPS)) + EPS) for g in range(SSD_GROUPS)]
    yb_ref[...] = (yg * jnp.where(group0, inv[0], inv[1]) * ng_ref[...]).astype(BF16)


def _mixab(uv, z, xbc, dt, p, l):
    S = uv.shape[0]
    L = CHUNK

    def rows(w_):
        return pl.BlockSpec((L, w_), lambda i: (i, 0))

    def per_layer(*shape):
        return pl.BlockSpec((None,) + shape, lambda i: (l,) + (0,) * len(shape))

    def const(*shape):
        return pl.BlockSpec(shape, lambda i: (0,) * len(shape))

    return pl.pallas_call(
        _mixab_kernel,
        grid=(S // L,),
        in_specs=[
            rows(2 * GMLP_W), rows(SSD_W), rows(SSD_XBC), rows(LANES),
            per_layer(1, GMLP_W), per_layer(GMLP_HEADS, L, L), per_layer(L, GMLP_W),
            per_layer(SSD_CONV, SSD_XBC), per_layer(1, SSD_XBC),
            per_layer(1, LANES), per_layer(1, LANES), per_layer(1, SSD_W), per_layer(1, SSD_W),
            const(2 * LANES, SSD_W), const(SSD_BC, SSD_W),
        ],
        out_specs=[rows(GMLP_W), rows(SSD_W)],
        out_shape=[jax.ShapeDtypeStruct((S, GMLP_W), BF16), jax.ShapeDtypeStruct((S, SSD_W), BF16)],
        scratch_shapes=[pltpu.VMEM((SSD_BC, SSD_W), F32), pltpu.VMEM((8, SSD_XBC), F32),
                        pltpu.VMEM((GMLP_HEADS, L, L), BF16)],
        compiler_params=pltpu.CompilerParams(
            dimension_semantics=("arbitrary",), vmem_limit_bytes=VMEM_LIMIT_V7X),
        name="mixab",
    )(uv, z, xbc, dt, p["vg"], p["ws"], p["bsf"], p["cw"], p["cb"], p["dtb"], p["alog"],
      p["dsk"], p["ng"], p["expand_heads2"], p["state_mask"])


def _mlaproj_kernel(cq_ref, ckv_ref, kr_ref, qg_ref, kvg_ref, wq_ref, wk_ref, wvt_ref,
                    ct_ref, s1_ref, s2_ref, q_out, k_out, vt_out):
    def rope(x, n):
        def wide(t):
            return t if n == 1 else jnp.concatenate([t] * n, axis=1)
        w = n * HEAD_SLAB
        up = pltpu.roll(x, w - MLA_ROPE // 2, 1)
        down = pltpu.roll(x, MLA_ROPE // 2, 1)
        return x * wide(ct_ref[...]) + up * wide(s1_ref[...]) + down * wide(s2_ref[...])

    qn = _rms(cq_ref[...].astype(F32), qg_ref[...]).astype(BF16)
    q = rope(_dot(qn, wq_ref[...]), MLA_HEADS) * (math.log2(math.e) / math.sqrt(MLA_NOPE + MLA_ROPE))
    q_out[...] = q.astype(BF16)

    kvn = _rms(ckv_ref[...].astype(F32), kvg_ref[...]).astype(BF16)
    kr = rope(kr_ref[...], 1)
    k = _dot(kvn, wk_ref[...]) + jnp.concatenate([kr] * MLA_HEADS, axis=1)
    k_out[...] = k.astype(BF16)
    srow = lax.broadcasted_iota(jnp.int32, (MLA_SLAB_W, 1), 0) & (HEAD_SLAB - 1)
    vt = _dot_nt(wvt_ref[...], kvn) + (srow == V_ONES_ROW).astype(F32)
    vt = vt.astype(BF16)
    for b in range(TM_PROJ // TK_ATT):
        vt_out[b] = vt[:, b * TK_ATT:(b + 1) * TK_ATT]


def _mlaproj(cq, ckv, kr, p, l, tabs):
    S = cq.shape[0]
    tm = TM_PROJ

    def rows(w_):
        return pl.BlockSpec((tm, w_), lambda i: (i, 0))

    def per_layer(*shape):
        return pl.BlockSpec((None,) + shape, lambda i: (l,) + (0,) * len(shape))

    out = jax.ShapeDtypeStruct((S, MLA_SLAB_W), BF16)
    nb = tm // TK_ATT
    return pl.pallas_call(
        _mlaproj_kernel,
        grid=(S // tm,),
        in_specs=[
            rows(MLA_Q_RANK), rows(MLA_KV_RANK), rows(LANES),
            per_layer(1, MLA_Q_RANK), per_layer(1, MLA_KV_RANK),
            per_layer(MLA_Q_RANK, MLA_SLAB_W), per_layer(MLA_KV_RANK, MLA_SLAB_W),
            per_layer(MLA_SLAB_W, MLA_KV_RANK),
            rows(LANES), rows(LANES), rows(LANES),
        ],
        out_specs=[rows(MLA_SLAB_W), rows(MLA_SLAB_W),
                   pl.BlockSpec((nb, MLA_SLAB_W, TK_ATT), lambda i: (i, 0, 0))],
        out_shape=[out, out, jax.ShapeDtypeStruct((S // TK_ATT, MLA_SLAB_W, TK_ATT), BF16)],
        compiler_params=pltpu.CompilerParams(
            dimension_semantics=("arbitrary",), vmem_limit_bytes=VMEM_LIMIT_V7X),
        name="mlaproj",
    )(cq, ckv, kr, p["qg"], p["kvg"], p["wq"], p["wk"], p["wvt"], *tabs)


def _flash_kernel(q_ref, k_ref, vt_ref, o_ref, sa_ref, sb_ref, m_ref, acc_ref):
    tq, tk = TQ_ATT, TK_ATT
    n_diag = tq // tk
    qi = pl.program_id(1)
    heads = (0, 1)
    bufs = (sa_ref, sb_ref)

    def slab(hh):
        return slice(hh * HEAD_SLAB, (hh + 1) * HEAD_SLAB)

    def scores(blk, s_ref, c0=0):
        start = pl.multiple_of(blk * tk, tk)
        for hh in heads:
            s_ref[hh, :, c0:] = _dot_nt(k_ref[pl.ds(start, tk), slab(hh)], q_ref[c0:, slab(hh)])

    def softmax_pv(blk, s_ref, c0=0, masked=False):
        for hh in heads:
            s = s_ref[hh, :, c0:]
            if masked:
                key = lax.broadcasted_iota(jnp.int32, s.shape, 0)
                qry = lax.broadcasted_iota(jnp.int32, s.shape, 1)
                s = jnp.where(key <= qry, s, -jnp.inf)
            m_old = m_ref[hh, :, c0:]
            m_new = jnp.maximum(m_old, jnp.max(s, axis=0, keepdims=True))
            alpha = jnp.exp2(m_old - m_new)
            p = jnp.exp2(s - m_new).astype(BF16)
            acc_ref[hh, :, c0:] = alpha * acc_ref[hh, :, c0:] + _dot(vt_ref[blk, slab(hh), :], p)
            m_ref[hh, :, c0:] = m_new

    m_ref[...] = jnp.full_like(m_ref, -jnp.inf)
    acc_ref[...] = jnp.zeros_like(acc_ref)
    scores(0, sa_ref)

    def body(j, carry):
        scores(2 * j + 1, sb_ref)
        softmax_pv(2 * j, sa_ref)
        scores(2 * j + 2, sa_ref)
        softmax_pv(2 * j + 1, sb_ref)
        return carry

    first_diag = qi * n_diag
    lax.fori_loop(0, first_diag // 2, body, 0)

    for d in range(n_diag):
        if d + 1 < n_diag:
            scores(first_diag + d + 1, bufs[(d + 1) % 2], (d + 1) * tk)
        softmax_pv(first_diag + d, bufs[d % 2], d * tk, masked=True)

    outs = []
    for hh in heads:
        acc = acc_ref[hh]
        outs.append((acc * (1.0 / acc[V_ONES_ROW:V_ONES_ROW + 1, :])).T)
    lane = lax.broadcasted_iota(jnp.int32, (1, HEAD_SLAB), 1)
    o_ref[...] = jnp.where(lane < MLA_V, outs[0], pltpu.roll(outs[1], MLA_V, 1)).astype(BF16)


def _flash(q, k, vt):
    S = q.shape[0]
    tq, tk = TQ_ATT, TK_ATT
    pair_w = 2 * HEAD_SLAB
    return pl.pallas_call(
        _flash_kernel,
        grid=(MLA_HEADS // 2, S // tq),
        in_specs=[
            pl.BlockSpec((tq, pair_w), lambda pr, i: (i, pr)),
            pl.BlockSpec((S, pair_w), lambda pr, i: (0, pr)),
            pl.BlockSpec((S // tk, pair_w, tk), lambda pr, i: (0, pr, 0)),
        ],
        out_specs=pl.BlockSpec((tq, 2 * MLA_V), lambda pr, i: (i, pr)),
        out_shape=jax.ShapeDtypeStruct((S, MLA_W), BF16),
        scratch_shapes=[
            pltpu.VMEM((2, tk, tq), F32), pltpu.VMEM((2, tk, tq), F32),
            pltpu.VMEM((2, 1, tq), F32), pltpu.VMEM((2, HEAD_SLAB, tq), F32),
        ],
        compiler_params=pltpu.CompilerParams(
            dimension_semantics=("arbitrary", "arbitrary"), vmem_limit_bytes=VMEM_LIMIT_V7X),
        name="flash",
    )(q, k, vt)


def _outmlp_kernel(x_ref, ya_ref, yb_ref, yc_ref, wo_ref, g_ref, w1_ref, w2_ref, fg_ref, o_ref,
                   hn_ref, *, final):
    c = pl.program_id(1)

    @pl.when(c == 0)
    def _():
        a0, a1, a2 = GMLP_W, GMLP_W + SSD_W, GMLP_W + SSD_W + MLA_W
        x = x_ref[...]
        x = x + _dot(ya_ref[...], wo_ref[0:a0, :])
        x = x + _dot(yb_ref[...], wo_ref[a0:a1, :])
        x = x + _dot(yc_ref[...], wo_ref[a1:a2, :])
        o_ref[...] = x
        hn_ref[...] = _rms(x, g_ref[...]).astype(BF16)

    h1 = _dot(hn_ref[...], w1_ref[...])
    h1 = jnp.square(jnp.maximum(h1, 0.0)).astype(BF16)
    o_ref[...] += _dot(h1, w2_ref[...])

    if final:
        @pl.when(c == pl.num_programs(1) - 1)
        def _():
            o_ref[...] = _rms(o_ref[...], fg_ref[...])


def _outmlp(x, ya, yb, yc, p, l, final_g, final):
    S = x.shape[0]
    tm = TM_MLP

    def rows(w_):
        return pl.BlockSpec((tm, w_), lambda i, c: (i, 0))

    def per_layer(*shape):
        return pl.BlockSpec((None,) + shape, lambda i, c: (l,) + (0,) * len(shape))

    return pl.pallas_call(
        functools.partial(_outmlp_kernel, final=final),
        grid=(S // tm, D_FF // FF_CHUNK),
        in_specs=[
            rows(D_MODEL), rows(GMLP_W), rows(SSD_W), rows(MLA_W),
            per_layer(D_MODEL, D_MODEL), per_layer(1, D_MODEL),
            pl.BlockSpec((None, D_MODEL, FF_CHUNK), lambda i, c: (l, 0, c)),
            pl.BlockSpec((None, FF_CHUNK, D_MODEL), lambda i, c: (l, c, 0)),
            pl.BlockSpec((1, D_MODEL), lambda i, c: (0, 0)),
        ],
        out_specs=rows(D_MODEL),
        out_shape=jax.ShapeDtypeStruct((S, D_MODEL), F32),
        scratch_shapes=[pltpu.VMEM((tm, D_MODEL), BF16)],
        compiler_params=pltpu.CompilerParams(
            dimension_semantics=("arbitrary", "arbitrary"), vmem_limit_bytes=VMEM_LIMIT_V7X),
        name="outmlp",
    )(x, ya, yb, yc, p["wo"], p["mlp_g"], p["w1"], p["w2"], final_g)


def _pad_lanes(a, n):
    return jnp.pad(a, [(0, 0)] * (a.ndim - 1) + [(0, n - a.shape[-1])])


def _head_slabs(w, heads, width, lane0):
    lead = w.shape[:-1]
    w = w.reshape(lead + (heads, width))
    w = jnp.pad(w, [(0, 0)] * len(lead) + [(0, 0), (lane0, HEAD_SLAB - lane0 - width)])
    return w.reshape(lead + (heads * HEAD_SLAB,))


def _layout_params(norm_mix_g, w_in, gmlp_v_norm_g, gmlp_w_s, gmlp_b_s, ssd_conv_w, ssd_conv_b,
                   ssd_dt_bias, ssd_a_log, ssd_d, ssd_norm_g, mla_q_norm_g, mla_w_qb,
                   mla_kv_norm_g, mla_w_kvb, w_out, norm_mlp_g, mlp_w1, mlp_w2):
    widths = (2 * GMLP_W, SSD_W, SSD_XBC, SSD_HEADS, MLA_Q_RANK, MLA_KV_RANK, MLA_ROPE)
    offs = np.cumsum((0,) + widths)
    uv, z, xbc, dt, cq, ckv, kr = [w_in[..., offs[i]:offs[i + 1]] for i in range(7)]
    kr = jnp.pad(kr, [(0, 0), (0, 0), (ROPE_LANE0, HEAD_SLAB - ROPE_LANE0 - MLA_ROPE)])
    w_in_p = jnp.concatenate([uv, z, xbc, cq, ckv, _pad_lanes(dt, LANES), kr], axis=-1).astype(BF16)

    qk_w = MLA_NOPE + MLA_ROPE
    wq = _head_slabs(mla_w_qb, MLA_HEADS, qk_w, 0)
    kv = mla_w_kvb.reshape(DEPTH, MLA_KV_RANK, MLA_HEADS, MLA_NOPE + MLA_V)
    wk = _head_slabs(kv[..., :MLA_NOPE].reshape(DEPTH, MLA_KV_RANK, -1), MLA_HEADS, MLA_NOPE, 0)
    wv = _head_slabs(kv[..., MLA_NOPE:].reshape(DEPTH, MLA_KV_RANK, -1), MLA_HEADS, MLA_V, 0)

    return {
        "mix_g": norm_mix_g[:, None, :],
        "w_in": w_in_p,
        "vg": gmlp_v_norm_g[:, None, :],
        "ws": gmlp_w_s,
        "bsf": jnp.repeat(jnp.swapaxes(gmlp_b_s, 1, 2), GMLP_HEAD_DIM, axis=-1),
        "cw": ssd_conv_w,
        "cb": ssd_conv_b[:, None, :],
        "dtb": _pad_lanes(ssd_dt_bias, LANES)[:, None, :],
        "alog": _pad_lanes(ssd_a_log, LANES)[:, None, :],
        "dsk": jnp.repeat(ssd_d, SSD_HEAD_DIM, axis=-1)[:, None, :],
        "ng": ssd_norm_g[:, None, :],
        "qg": mla_q_norm_g[:, None, :],
        "kvg": mla_kv_norm_g[:, None, :],
        "wq": wq.astype(BF16),
        "wk": wk.astype(BF16),
        "wvt": jnp.swapaxes(wv, 1, 2).astype(BF16),
        "wo": w_out.astype(BF16),
        "mlp_g": norm_mlp_g[:, None, :],
        "w1": mlp_w1.astype(BF16),
        "w2": mlp_w2.astype(BF16),
    }


def _constant_tables():
    heads = np.arange(SSD_W) // SSD_HEAD_DIM
    expand_heads = (np.arange(LANES)[:, None] == heads[None, :]).astype(np.float32)
    groups = np.arange(SSD_W) // (SSD_W // SSD_GROUPS)
    state_mask = ((np.arange(SSD_BC) // SSD_STATE)[:, None] == groups[None, :]).astype(np.float32)
    return {
        "expand_heads2": jnp.asarray(np.concatenate([expand_heads, expand_heads], axis=0), dtype=BF16),
        "state_mask": jnp.asarray(state_mask),
    }


def _rope_tables(S):
    half = MLA_ROPE // 2
    pos = jnp.arange(S, dtype=F32)
    inv_freq = jnp.power(ROPE_BASE, -jnp.arange(half, dtype=F32) / half)
    ang = pos[:, None] * inv_freq[None, :]
    cos, sin = jnp.cos(ang), jnp.sin(ang)
    zeros = jnp.zeros((S, half), F32)
    pad = jnp.zeros((S, HEAD_SLAB - ROPE_LANE0 - MLA_ROPE), F32)
    ct = jnp.concatenate([jnp.ones((S, ROPE_LANE0), F32), cos, cos, pad], axis=1)
    s1 = jnp.concatenate([jnp.zeros((S, ROPE_LANE0), F32), -sin, zeros, pad], axis=1)
    s2 = jnp.concatenate([jnp.zeros((S, ROPE_LANE0), F32), zeros, sin, pad], axis=1)
    return ct, s1, s2


def kernel(x, norm_mix_g, w_in, gmlp_v_norm_g, gmlp_w_s, gmlp_b_s, ssd_conv_w, ssd_conv_b, ssd_dt_bias, ssd_a_log, ssd_d, ssd_norm_g, mla_q_norm_g, mla_w_qb, mla_kv_norm_g, mla_w_kvb, w_out, norm_mlp_g, mlp_w1, mlp_w2, final_norm_g):
    B, S, _ = x.shape
    assert B == 1 and S % TQ_ATT == 0 and S % TM_PROJ == 0 and S % TM_MLP == 0
    p = _layout_params(norm_mix_g, w_in, gmlp_v_norm_g, gmlp_w_s, gmlp_b_s, ssd_conv_w, ssd_conv_b,
                       ssd_dt_bias, ssd_a_log, ssd_d, ssd_norm_g, mla_q_norm_g, mla_w_qb,
                       mla_kv_norm_g, mla_w_kvb, w_out, norm_mlp_g, mlp_w1, mlp_w2)
    p.update(_constant_tables())
    tabs = _rope_tables(S)
    fg = final_norm_g[None, :]
    xs = x.reshape(S, D_MODEL)
    for l in range(DEPTH):
        uv, z, xbc, cq, ckv, dt, kr = _inproj(xs, p["mix_g"], p["w_in"], l)
        ya, yb = _mixab(uv, z, xbc, dt, p, l)
        q, k, vt = _mlaproj(cq, ckv, kr, p, l, tabs)
        yc = _flash(q, k, vt)
        xs = _outmlp(xs, ya, yb, yc, p, l, fg, final=(l == DEPTH - 1))
    return xs.reshape(B, S, D_MODEL)
```

```python
import functools
import math

import numpy as np
import jax
import jax.numpy as jnp
from jax import lax
from jax.experimental import pallas as pl
from jax.experimental.pallas import tpu as pltpu

F32 = jnp.float32
BF16 = jnp.bfloat16
HIGHEST = lax.Precision.HIGHEST

D_MODEL = 1024
DEPTH = 4
CHUNK = 128
GMLP_HEADS = 4
GMLP_HEAD_DIM = 64
GMLP_W = GMLP_HEADS * GMLP_HEAD_DIM
SSD_HEADS = 6
SSD_HEAD_DIM = 64
SSD_W = SSD_HEADS * SSD_HEAD_DIM
SSD_GROUPS = 2
SSD_STATE = 64
SSD_CONV = 4
SSD_BC = SSD_GROUPS * SSD_STATE
SSD_XBC = SSD_W + 2 * SSD_BC
MLA_HEADS = 6
MLA_NOPE = 64
MLA_ROPE = 32
MLA_V = 64
MLA_W = MLA_HEADS * MLA_V
MLA_Q_RANK = 384
MLA_KV_RANK = 256
ROPE_BASE = 10000.0
D_FF = 4 * D_MODEL
EPS = 1e-6

LANES = 128
HEAD_SLAB = LANES
MLA_SLAB_W = MLA_HEADS * HEAD_SLAB
V_ONES_ROW = MLA_V
ROPE_LANE0 = MLA_NOPE
VMEM_LIMIT_V7X = 56 * 1024 * 1024

_OFF_UV = 0
_OFF_Z = _OFF_UV + 2 * GMLP_W
_OFF_XBC = _OFF_Z + SSD_W
_OFF_CQ = _OFF_XBC + SSD_XBC
_OFF_CKV = _OFF_CQ + MLA_Q_RANK
_OFF_DT = _OFF_CKV + MLA_KV_RANK
_OFF_KR = _OFF_DT + LANES
D_IN_PAD = _OFF_KR + LANES

TM_PROJ = 1024
TM_MLP = 1024
TQ_ATT = 1024
TK_ATT = 512
FF_CHUNK = 1024


def _rms(x, g):
    var = jnp.mean(x * x, axis=-1, keepdims=True)
    return (x * lax.rsqrt(var + EPS)) * g


def _sigmoid(x):
    return 1.0 / (1.0 + jnp.exp(-x))


def _lane_group(width, group_width):
    shift = group_width.bit_length() - 1
    assert 1 << shift == group_width
    return lax.shift_right_logical(lax.broadcasted_iota(jnp.int32, (1, width), 1), shift)


def _dot(a, b, precision=None):
    return jnp.dot(a, b, preferred_element_type=F32, precision=precision)


def _dot_nt(a, b):
    return lax.dot_general(a, b, (((1,), (1,)), ((), ())), preferred_element_type=F32)


def _inproj_kernel(x_ref, g_ref, w_ref, uv_ref, z_ref, xbc_ref, cq_ref, ckv_ref, dt_ref, kr_ref):
    h = _rms(x_ref[...], g_ref[...]).astype(BF16)

    def seg(a, b):
        return _dot(h, w_ref[:, a:b])

    uv_ref[...] = seg(_OFF_UV, _OFF_Z).astype(BF16)
    z_ref[...] = seg(_OFF_Z, _OFF_XBC).astype(BF16)
    xbc_ref[...] = seg(_OFF_XBC, _OFF_CQ).astype(BF16)
    cq_ref[...] = seg(_OFF_CQ, _OFF_CKV).astype(BF16)
    ckv_ref[...] = seg(_OFF_CKV, _OFF_DT).astype(BF16)
    dt_ref[...] = seg(_OFF_DT, _OFF_KR)
    kr_ref[...] = seg(_OFF_KR, D_IN_PAD)


def _inproj(x, g, w, l):
    S = x.shape[0]
    tm = TM_PROJ
    widths = (2 * GMLP_W, SSD_W, SSD_XBC, MLA_Q_RANK, MLA_KV_RANK, LANES, LANES)
    dtypes = (BF16, BF16, BF16, BF16, BF16, F32, F32)
    return pl.pallas_call(
        _inproj_kernel,
        grid=(S // tm,),
        in_specs=[
            pl.BlockSpec((tm, D_MODEL), lambda i: (i, 0)),
            pl.BlockSpec((None, 1, D_MODEL), lambda i: (l, 0, 0)),
            pl.BlockSpec((None, D_MODEL, D_IN_PAD), lambda i: (l, 0, 0)),
        ],
        out_specs=[pl.BlockSpec((tm, w_), lambda i: (i, 0)) for w_ in widths],
        out_shape=[jax.ShapeDtypeStruct((S, w_), d_) for w_, d_ in zip(widths, dtypes)],
        compiler_params=pltpu.CompilerParams(
            dimension_semantics=("arbitrary",), vmem_limit_bytes=VMEM_LIMIT_V7X),
        name="inproj",
    )(x, g, w)


def _mixab_kernel(uv_ref, z_ref, xbc_ref, dt_ref, vg_ref, ws_ref, bsf_ref, cw_ref, cb_ref,
                  dtb_ref, alog_ref, dsk_ref, ng_ref, e2_ref, bd_ref,
                  ya_ref, yb_ref, state_ref, tail_ref, wm_ref):
    L = CHUNK
    row = lax.broadcasted_iota(jnp.int32, (L, L), 0)
    col = lax.broadcasted_iota(jnp.int32, (L, L), 1)
    causal = col <= row

    @pl.when(pl.program_id(0) == 0)
    def _():
        state_ref[...] = jnp.zeros_like(state_ref)
        tail_ref[...] = jnp.zeros_like(tail_ref)
        for h in range(GMLP_HEADS):
            wm_ref[h] = jnp.where(causal, ws_ref[h], 0.0).astype(BF16)

    uv = uv_ref[...].astype(F32)
    guv = 0.5 * uv * (1.0 + jnp.tanh(math.sqrt(2.0 / math.pi) * (uv + 0.044715 * (uv * uv * uv))))
    u = guv[:, :GMLP_W]
    vn = _rms(guv[:, GMLP_W:], vg_ref[...])
    head_a = _lane_group(GMLP_W, GMLP_HEAD_DIM)
    mixed = bsf_ref[...]
    for h in range(GMLP_HEADS):
        vm = jnp.where(head_a == h, vn, 0.0).astype(BF16)
        mixed = mixed + _dot(wm_ref[h], vm)
    ya_ref[...] = (u * mixed).astype(BF16)

    x = xbc_ref[...].astype(F32)
    xfull = jnp.concatenate([tail_ref[...], x], axis=0)
    conv = cb_ref[...] + cw_ref[SSD_CONV - 1:SSD_CONV, :] * x
    for k in range(SSD_CONV - 1):
        shifted = pltpu.roll(xfull, SSD_CONV - 1 - k, 0)[8:8 + L]
        conv = conv + cw_ref[k:k + 1, :] * shifted
    tail_ref[...] = x[L - 8:L]
    xc = conv * _sigmoid(conv)
    xs = xc[:, :SSD_W]
    Bm = xc[:, SSD_W:SSD_W + SSD_BC]
    Cm = xc[:, SSD_W + SSD_BC:]

    dtr = dt_ref[...] + dtb_ref[...]
    dt = jnp.maximum(dtr, 0.0) + jnp.log1p(jnp.exp(-jnp.abs(dtr)))
    a_col = dt * -jnp.exp(alog_ref[...])
    for k in range(CHUNK.bit_length() - 1):
        a_col = a_col + jnp.where(row >= (1 << k), pltpu.roll(a_col, 1 << k, 0), 0.0)
    a_row = a_col.T
    dt_row = dt.T

    Bb = Bm.astype(BF16)
    Cb = Cm.astype(BF16)
    hpg = SSD_HEADS // SSD_GROUPS
    group_bc = _lane_group(SSD_BC, SSD_STATE)
    cb = [_dot_nt(jnp.where(group_bc == g, Cm, 0.0).astype(BF16), Bb) for g in range(SSD_GROUPS)]
    first_half = lax.broadcasted_iota(jnp.int32, (1, LANES), 1) < SSD_HEAD_DIM
    slabs = []
    for j in range(SSD_W // LANES):
        xsl = xs[:, j * LANES:(j + 1) * LANES]
        acc = None
        for hh in range(LANES // SSD_HEAD_DIM):
            h = j * (LANES // SSD_HEAD_DIM) + hh
            seg = a_col[:, h:h + 1] - a_row[h:h + 1, :]
            decay = jnp.exp(jnp.where(causal, seg, -jnp.inf))
            w = (cb[h // hpg] * decay * dt_row[h:h + 1, :]).astype(BF16)
            xm = jnp.where(first_half == (hh == 0), xsl, 0.0).astype(BF16)
            term = _dot(w, xm)
            acc = term if acc is None else acc + term
        slabs.append(acc)
    y = xs * dsk_ref[...] + jnp.concatenate(slabs, axis=1)

    def expand_heads(v):
        hi = v.astype(BF16)
        lo = (v - hi.astype(F32)).astype(BF16)
        return _dot(jnp.concatenate([hi, lo], axis=1), e2_ref[...])

    ea_w = expand_heads(jnp.exp(a_col))
    dte_w = expand_heads(jnp.exp(a_col[L - 1:L, :] - a_col) * dt)
    s_prev = state_ref[...]
    y = y + _dot(Cb, s_prev.astype(BF16)) * ea_w
    xsd = (xs * dte_w).astype(BF16)
    bt = Bm.T.astype(BF16)
    state_ref[...] = ea_w[L - 1:L, :] * s_prev + bd_ref[...] * _dot(bt, xsd)

    zf = z_ref[...].astype(F32)
    yg = y * (zf * _sigmoid(zf))
    yg2 = yg * yg
    group0 = lax.broadcasted_iota(jnp.int32, (1, SSD_W), 1) < SSD_W // SSD_GROUPS
    inv = [lax.rsqrt(jnp.sum(jnp.where(group0 == (g == 0), yg2, 0.0), axis=-1, keepdims=True)
                     * (1.0 / (SSD_W // SSD_GROUPS)) + EPS) for g in range(SSD_GROUPS)]
    yb_ref[...] = (yg * jnp.where(group0, inv[0], inv[1]) * ng_ref[...]).astype(BF16)


def _mixab(uv, z, xbc, dt, p, l):
    S = uv.shape[0]
    L = CHUNK

    def rows(w_):
        return pl.BlockSpec((L, w_), lambda i: (i, 0))

    def per_layer(*shape):
        return pl.BlockSpec((None,) + shape, lambda i: (l,) + (0,) * len(shape))

    def const(*shape):
        return pl.BlockSpec(shape, lambda i: (0,) * len(shape))

    return pl.pallas_call(
        _mixab_kernel,
        grid=(S // L,),
        in_specs=[
            rows(2 * GMLP_W), rows(SSD_W), rows(SSD_XBC), rows(LANES),
            per_layer(1, GMLP_W), per_layer(GMLP_HEADS, L, L), per_layer(L, GMLP_W),
            per_layer(SSD_CONV, SSD_XBC), per_layer(1, SSD_XBC),
            per_layer(1, LANES), per_layer(1, LANES), per_layer(1, SSD_W), per_layer(1, SSD_W),
            const(2 * LANES, SSD_W), const(SSD_BC, SSD_W),
        ],
        out_specs=[rows(GMLP_W), rows(SSD_W)],
        out_shape=[jax.ShapeDtypeStruct((S, GMLP_W), BF16), jax.ShapeDtypeStruct((S, SSD_W), BF16)],
        scratch_shapes=[pltpu.VMEM((SSD_BC, SSD_W), F32), pltpu.VMEM((8, SSD_XBC), F32),
                        pltpu.VMEM((GMLP_HEADS, L, L), BF16)],
        compiler_params=pltpu.CompilerParams(
            dimension_semantics=("arbitrary",), vmem_limit_bytes=VMEM_LIMIT_V7X),
        name="mixab",
    )(uv, z, xbc, dt, p["vg"], p["ws"], p["bsf"], p["cw"], p["cb"], p["dtb"], p["alog"],
      p["dsk"], p["ng"], p["expand_heads2"], p["state_mask"])


def _mlaproj_kernel(cq_ref, ckv_ref, kr_ref, qg_ref, kvg_ref, wq_ref, wk_ref, wvt_ref,
                    ct_ref, s1_ref, s2_ref, q_out, k_out, vt_out):
    def rope(x, n):
        def wide(t):
            return t if n == 1 else jnp.concatenate([t] * n, axis=1)
        w = n * HEAD_SLAB
        up = pltpu.roll(x, w - MLA_ROPE // 2, 1)
        down = pltpu.roll(x, MLA_ROPE // 2, 1)
        return x * wide(ct_ref[...]) + up * wide(s1_ref[...]) + down * wide(s2_ref[...])

    qn = _rms(cq_ref[...].astype(F32), qg_ref[...]).astype(BF16)
    q = rope(_dot(qn, wq_ref[...]), MLA_HEADS) * (math.log2(math.e) / math.sqrt(MLA_NOPE + MLA_ROPE))
    q_out[...] = q.astype(BF16)

    kvn = _rms(ckv_ref[...].astype(F32), kvg_ref[...]).astype(BF16)
    kr = rope(kr_ref[...], 1)
    k = _dot(kvn, wk_ref[...]) + jnp.concatenate([kr] * MLA_HEADS, axis=1)
    k_out[...] = k.astype(BF16)
    srow = lax.broadcasted_iota(jnp.int32, (MLA_SLAB_W, 1), 0) & (HEAD_SLAB - 1)
    vt = _dot_nt(wvt_ref[...], kvn) + (srow == V_ONES_ROW).astype(F32)
    vt = vt.astype(BF16)
    for b in range(TM_PROJ // TK_ATT):
        vt_out[b] = vt[:, b * TK_ATT:(b + 1) * TK_ATT]


def _mlaproj(cq, ckv, kr, p, l, tabs):
    S = cq.shape[0]
    tm = TM_PROJ

    def rows(w_):
        return pl.BlockSpec((tm, w_), lambda i: (i, 0))

    def per_layer(*shape):
        return pl.BlockSpec((None,) + shape, lambda i: (l,) + (0,) * len(shape))

    out = jax.ShapeDtypeStruct((S, MLA_SLAB_W), BF16)
    nb = tm // TK_ATT
    return pl.pallas_call(
        _mlaproj_kernel,
        grid=(S // tm,),
        in_specs=[
            rows(MLA_Q_RANK), rows(MLA_KV_RANK), rows(LANES),
            per_layer(1, MLA_Q_RANK), per_layer(1, MLA_KV_RANK),
            per_layer(MLA_Q_RANK, MLA_SLAB_W), per_layer(MLA_KV_RANK, MLA_SLAB_W),
            per_layer(MLA_SLAB_W, MLA_KV_RANK),
            rows(LANES), rows(LANES), rows(LANES),
        ],
        out_specs=[rows(MLA_SLAB_W), rows(MLA_SLAB_W),
                   pl.BlockSpec((nb, MLA_SLAB_W, TK_ATT), lambda i: (i, 0, 0))],
        out_shape=[out, out, jax.ShapeDtypeStruct((S // TK_ATT, MLA_SLAB_W, TK_ATT), BF16)],
        compiler_params=pltpu.CompilerParams(
            dimension_semantics=("arbitrary",), vmem_limit_bytes=VMEM_LIMIT_V7X),
        name="mlaproj",
    )(cq, ckv, kr, p["qg"], p["kvg"], p["wq"], p["wk"], p["wvt"], *tabs)


def _flash_kernel(q_ref, k_ref, vt_ref, o_ref, sa_ref, sb_ref, m_ref, acc_ref):
    tq, tk = TQ_ATT, TK_ATT
    n_diag = tq // tk
    qi = pl.program_id(1)
    heads = (0, 1)
    bufs = (sa_ref, sb_ref)

    def slab(hh):
        return slice(hh * HEAD_SLAB, (hh + 1) * HEAD_SLAB)

    def scores(blk, s_ref, c0=0):
        start = pl.multiple_of(blk * tk, tk)
        for hh in heads:
            s_ref[hh, :, c0:] = _dot_nt(k_ref[pl.ds(start, tk), slab(hh)], q_ref[c0:, slab(hh)])

    def softmax_pv(blk, s_ref, c0=0, masked=False):
        for hh in heads:
            s = s_ref[hh, :, c0:]
            if masked:
                key = lax.broadcasted_iota(jnp.int32, s.shape, 0)
                qry = lax.broadcasted_iota(jnp.int32, s.shape, 1)
                s = jnp.where(key <= qry, s, -jnp.inf)
            m_old = m_ref[hh, :, c0:]
            m_new = jnp.maximum(m_old, jnp.max(s, axis=0, keepdims=True))
            alpha = jnp.exp2(m_old - m_new)
            p = jnp.exp2(s - m_new).astype(BF16)
            acc_ref[hh, :, c0:] = alpha * acc_ref[hh, :, c0:] + _dot(vt_ref[blk, slab(hh), :], p)
            m_ref[hh, :, c0:] = m_new

    m_ref[...] = jnp.full_like(m_ref, -jnp.inf)
    acc_ref[...] = jnp.zeros_like(acc_ref)
    scores(0, sa_ref)

    def body(j, carry):
        scores(2 * j + 1, sb_ref)
        softmax_pv(2 * j, sa_ref)
        scores(2 * j + 2, sa_ref)
        softmax_pv(2 * j + 1, sb_ref)
        return carry

    first_diag = qi * n_diag
    lax.fori_loop(0, first_diag // 2, body, 0)

    for d in range(n_diag):
        if d + 1 < n_diag:
            scores(first_diag + d + 1, bufs[(d + 1) % 2], (d + 1) * tk)
        softmax_pv(first_diag + d, bufs[d % 2], d * tk, masked=True)

    outs = []
    for hh in heads:
        acc = acc_ref[hh]
        outs.append((acc * (1.0 / acc[V_ONES_ROW:V_ONES_ROW + 1, :])).T)
    lane = lax.broadcasted_iota(jnp.int32, (1, HEAD_SLAB), 1)
    o_ref[...] = jnp.where(lane < MLA_V, outs[0], pltpu.roll(outs[1], MLA_V, 1)).astype(BF16)


def _flash(q, k, vt):
    S = q.shape[0]
    tq, tk = TQ_ATT, TK_ATT
    pair_w = 2 * HEAD_SLAB
    return pl.pallas_call(
        _flash_kernel,
        grid=(MLA_HEADS // 2, S // tq),
        in_specs=[
            pl.BlockSpec((tq, pair_w), lambda pr, i: (i, pr)),
            pl.BlockSpec((S, pair_w), lambda pr, i: (0, pr)),
            pl.BlockSpec((S // tk, pair_w, tk), lambda pr, i: (0, pr, 0)),
        ],
        out_specs=pl.BlockSpec((tq, 2 * MLA_V), lambda pr, i: (i, pr)),
        out_shape=jax.ShapeDtypeStruct((S, MLA_W), BF16),
        scratch_shapes=[
            pltpu.VMEM((2, tk, tq), F32), pltpu.VMEM((2, tk, tq), F32),
            pltpu.VMEM((2, 1, tq), F32), pltpu.VMEM((2, HEAD_SLAB, tq), F32),
        ],
        compiler_params=pltpu.CompilerParams(
            dimension_semantics=("arbitrary", "arbitrary"), vmem_limit_bytes=VMEM_LIMIT_V7X),
        name="flash",
    )(q, k, vt)


def _outmlp_kernel(x_ref, ya_ref, yb_ref, yc_ref, wo_ref, g_ref, w1_ref, w2_ref, fg_ref, o_ref,
                   hn_ref, *, final):
    c = pl.program_id(1)

    @pl.when(c == 0)
    def _():
        a0, a1, a2 = GMLP_W, GMLP_W + SSD_W, GMLP_W + SSD_W + MLA_W
        x = x_ref[...]
        x = x + _dot(ya_ref[...], wo_ref[0:a0, :])
        x = x + _dot(yb_ref[...], wo_ref[a0:a1, :])
        x = x + _dot(yc_ref[...], wo_ref[a1:a2, :])
        o_ref[...] = x
        hn_ref[...] = _rms(x, g_ref[...]).astype(BF16)

    h1 = _dot(hn_ref[...], w1_ref[...])
    h1 = jnp.square(jnp.maximum(h1, 0.0)).astype(BF16)
    o_ref[...] += _dot(h1, w2_ref[...])

    if final:
        @pl.when(c == pl.num_programs(1) - 1)
        def _():
            o_ref[...] = _rms(o_ref[...], fg_ref[...])


def _outmlp(x, ya, yb, yc, p, l, final_g, final):
    S = x.shape[0]
    tm = TM_MLP

    def rows(w_):
        return pl.BlockSpec((tm, w_), lambda i, c: (i, 0))

    def per_layer(*shape):
        return pl.BlockSpec((None,) + shape, lambda i, c: (l,) + (0,) * len(shape))

    return pl.pallas_call(
        functools.partial(_outmlp_kernel, final=final),
        grid=(S // tm, D_FF // FF_CHUNK),
        in_specs=[
            rows(D_MODEL), rows(GMLP_W), rows(SSD_W), rows(MLA_W),
            per_layer(D_MODEL, D_MODEL), per_layer(1, D_MODEL),
            pl.BlockSpec((None, D_MODEL, FF_CHUNK), lambda i, c: (l, 0, c)),
            pl.BlockSpec((None, FF_CHUNK, D_MODEL), lambda i, c: (l, c, 0)),
            pl.BlockSpec((1, D_MODEL), lambda i, c: (0, 0)),
        ],
        out_specs=rows(D_MODEL),
        out_shape=jax.ShapeDtypeStruct((S, D_MODEL), F32),
        scratch_shapes=[pltpu.VMEM((tm, D_MODEL), BF16)],
        compiler_params=pltpu.CompilerParams(
            dimension_semantics=("arbitrary", "arbitrary"), vmem_limit_bytes=VMEM_LIMIT_V7X),
        name="outmlp",
    )(x, ya, yb, yc, p["wo"], p["mlp_g"], p["w1"], p["w2"], final_g)


def _pad_lanes(a, n):
    return jnp.pad(a, [(0, 0)] * (a.ndim - 1) + [(0, n - a.shape[-1])])


def _head_slabs(w, heads, width, lane0):
    lead = w.shape[:-1]
    w = w.reshape(lead + (heads, width))
    w = jnp.pad(w, [(0, 0)] * len(lead) + [(0, 0), (lane0, HEAD_SLAB - lane0 - width)])
    return w.reshape(lead + (heads * HEAD_SLAB,))


def _layout_params(norm_mix_g, w_in, gmlp_v_norm_g, gmlp_w_s, gmlp_b_s, ssd_conv_w, ssd_conv_b,
                   ssd_dt_bias, ssd_a_log, ssd_d, ssd_norm_g, mla_q_norm_g, mla_w_qb,
                   mla_kv_norm_g, mla_w_kvb, w_out, norm_mlp_g, mlp_w1, mlp_w2):
    widths = (2 * GMLP_W, SSD_W, SSD_XBC, SSD_HEADS, MLA_Q_RANK, MLA_KV_RANK, MLA_ROPE)
    offs = np.cumsum((0,) + widths)
    uv, z, xbc, dt, cq, ckv, kr = [w_in[..., offs[i]:offs[i + 1]] for i in range(7)]
    kr = jnp.pad(kr, [(0, 0), (0, 0), (ROPE_LANE0, HEAD_SLAB - ROPE_LANE0 - MLA_ROPE)])
    w_in_p = jnp.concatenate([uv, z, xbc, cq, ckv, _pad_lanes(dt, LANES), kr], axis=-1).astype(BF16)

    qk_w = MLA_NOPE + MLA_ROPE
    wq = _head_slabs(mla_w_qb, MLA_HEADS, qk_w, 0)
    kv = mla_w_kvb.reshape(DEPTH, MLA_KV_RANK, MLA_HEADS, MLA_NOPE + MLA_V)
    wk = _head_slabs(kv[..., :MLA_NOPE].reshape(DEPTH, MLA_KV_RANK, -1), MLA_HEADS, MLA_NOPE, 0)
    wv = _head_slabs(kv[..., MLA_NOPE:].reshape(DEPTH, MLA_KV_RANK, -1), MLA_HEADS, MLA_V, 0)

    return {
        "mix_g": norm_mix_g[:, None, :],
        "w_in": w_in_p,
        "vg": gmlp_v_norm_g[:, None, :],
        "ws": gmlp_w_s,
        "bsf": jnp.repeat(jnp.swapaxes(gmlp_b_s, 1, 2), GMLP_HEAD_DIM, axis=-1),
        "cw": ssd_conv_w,
        "cb": ssd_conv_b[:, None, :],
        "dtb": _pad_lanes(ssd_dt_bias, LANES)[:, None, :],
        "alog": _pad_lanes(ssd_a_log, LANES)[:, None, :],
        "dsk": jnp.repeat(ssd_d, SSD_HEAD_DIM, axis=-1)[:, None, :],
        "ng": ssd_norm_g[:, None, :],
        "qg": mla_q_norm_g[:, None, :],
        "kvg": mla_kv_norm_g[:, None, :],
        "wq": wq.astype(BF16),
        "wk": wk.astype(BF16),
        "wvt": jnp.swapaxes(wv, 1, 2).astype(BF16),
        "wo": w_out.astype(BF16),
        "mlp_g": norm_mlp_g[:, None, :],
        "w1": mlp_w1.astype(BF16),
        "w2": mlp_w2.astype(BF16),
    }


def _constant_tables():
    heads = np.arange(SSD_W) // SSD_HEAD_DIM
    expand_heads = (np.arange(LANES)[:, None] == heads[None, :]).astype(np.float32)
    groups = np.arange(SSD_W) // (SSD_W // SSD_GROUPS)
    state_mask = ((np.arange(SSD_BC) // SSD_STATE)[:, None] == groups[None, :]).astype(np.float32)
    return {
        "expand_heads2": jnp.asarray(np.concatenate([expand_heads, expand_heads], axis=0), dtype=BF16),
        "state_mask": jnp.asarray(state_mask),
    }


def _rope_tables(S):
    half = MLA_ROPE // 2
    pos = jnp.arange(S, dtype=F32)
    inv_freq = jnp.power(ROPE_BASE, -jnp.arange(half, dtype=F32) / half)
    ang = pos[:, None] * inv_freq[None, :]
    cos, sin = jnp.cos(ang), jnp.sin(ang)
    zeros = jnp.zeros((S, half), F32)
    pad = jnp.zeros((S, HEAD_SLAB - ROPE_LANE0 - MLA_ROPE), F32)
    ct = jnp.concatenate([jnp.ones((S, ROPE_LANE0), F32), cos, cos, pad], axis=1)
    s1 = jnp.concatenate([jnp.zeros((S, ROPE_LANE0), F32), -sin, zeros, pad], axis=1)
    s2 = jnp.concatenate([jnp.zeros((S, ROPE_LANE0), F32), zeros, sin, pad], axis=1)
    return ct, s1, s2


def kernel(x, norm_mix_g, w_in, gmlp_v_norm_g, gmlp_w_s, gmlp_b_s, ssd_conv_w, ssd_conv_b, ssd_dt_bias, ssd_a_log, ssd_d, ssd_norm_g, mla_q_norm_g, mla_w_qb, mla_kv_norm_g, mla_w_kvb, w_out, norm_mlp_g, mlp_w1, mlp_w2, final_norm_g):
    B, S, _ = x.shape
    assert B == 1 and S % TQ_ATT == 0 and S % TM_PROJ == 0 and S % TM_MLP == 0
    p = _layout_params(norm_mix_g, w_in, gmlp_v_norm_g, gmlp_w_s, gmlp_b_s, ssd_conv_w, ssd_conv_b,
                       ssd_dt_bias, ssd_a_log, ssd_d, ssd_norm_g, mla_q_norm_g, mla_w_qb,
                       mla_kv_norm_g, mla_w_kvb, w_out, norm_mlp_g, mlp_w1, mlp_w2)
    p.update(_constant_tables())
    tabs = _rope_tables(S)
    fg = final_norm_g[None, :]
    xs = x.reshape(S, D_MODEL)
    for l in range(DEPTH):
        uv, z, xbc, cq, ckv, dt, kr = _inproj(xs, p["mix_g"], p["w_in"], l)
        ya, yb = _mixab(uv, z, xbc, dt, p, l)
        q, k, vt = _mlaproj(cq, ckv, kr, p, l, tabs)
        yc = _flash(q, k, vt)
        xs = _outmlp(xs, ya, yb, yc, p, l, fg, final=(l == DEPTH - 1))
    return xs.reshape(B, S, D_MODEL)
```

```python
import functools
import math

import numpy as np
import jax
import jax.numpy as jnp
from jax import lax
from jax.experimental import pallas as pl
from jax.experimental.pallas import tpu as pltpu

F32 = jnp.float32
BF16 = jnp.bfloat16
HIGHEST = lax.Precision.HIGHEST

D_MODEL = 1024
DEPTH = 4
CHUNK = 128
GMLP_HEADS = 4
GMLP_HEAD_DIM = 64
GMLP_W = GMLP_HEADS * GMLP_HEAD_DIM
SSD_HEADS = 6
SSD_HEAD_DIM = 64
SSD_W = SSD_HEADS * SSD_HEAD_DIM
SSD_GROUPS = 2
SSD_STATE = 64
SSD_CONV = 4
SSD_BC = SSD_GROUPS * SSD_STATE
SSD_XBC = SSD_W + 2 * SSD_BC
MLA_HEADS = 6
MLA_NOPE = 64
MLA_ROPE = 32
MLA_V = 64
MLA_W = MLA_HEADS * MLA_V
MLA_Q_RANK = 384
MLA_KV_RANK = 256
ROPE_BASE = 10000.0
D_FF = 4 * D_MODEL
EPS = 1e-6

LANES = 128
SUBLANES = 8
HEAD_SLAB = LANES
MLA_SLAB_W = MLA_HEADS * HEAD_SLAB
V_ONES_ROW = MLA_V
ROPE_LANE0 = MLA_NOPE
VMEM_LIMIT_V7X = 56 * 1024 * 1024

_OFF_UV = 0
_OFF_Z = _OFF_UV + 2 * GMLP_W
_OFF_XBC = _OFF_Z + SSD_W
_OFF_CQ = _OFF_XBC + SSD_XBC
_OFF_CKV = _OFF_CQ + MLA_Q_RANK
_OFF_DT = _OFF_CKV + MLA_KV_RANK
_OFF_KR = _OFF_DT + LANES
D_IN_PAD = _OFF_KR + LANES

TM_PROJ = 1024
TM_MLP = 1024
TQ_ATT = 1024
TK_ATT = 512
FF_CHUNK = 1024


def _rms(x, g):
    var = jnp.mean(x * x, axis=-1, keepdims=True)
    return (x * lax.rsqrt(var + EPS)) * g


def _sigmoid(x):
    return 1.0 / (1.0 + jnp.exp(-x))


def _lane_group(width, group_width):
    shift = group_width.bit_length() - 1
    assert 1 << shift == group_width
    return lax.shift_right_logical(lax.broadcasted_iota(jnp.int32, (1, width), 1), shift)


def _dot(a, b, precision=None):
    return jnp.dot(a, b, preferred_element_type=F32, precision=precision)


def _dot_nt(a, b):
    return lax.dot_general(a, b, (((1,), (1,)), ((), ())), preferred_element_type=F32)


def _inproj_kernel(x_ref, g_ref, w_ref, uv_ref, z_ref, xbc_ref, cq_ref, ckv_ref, dt_ref, kr_ref):
    h = _rms(x_ref[...], g_ref[...]).astype(BF16)

    def seg(a, b):
        return _dot(h, w_ref[:, a:b])

    uv_ref[...] = seg(_OFF_UV, _OFF_Z).astype(BF16)
    z_ref[...] = seg(_OFF_Z, _OFF_XBC).astype(BF16)
    xbc_ref[...] = seg(_OFF_XBC, _OFF_CQ).astype(BF16)
    cq_ref[...] = seg(_OFF_CQ, _OFF_CKV).astype(BF16)
    ckv_ref[...] = seg(_OFF_CKV, _OFF_DT).astype(BF16)
    dt_ref[...] = seg(_OFF_DT, _OFF_KR)
    kr_ref[...] = seg(_OFF_KR, D_IN_PAD)


def _inproj(x, g, w, l):
    S = x.shape[0]
    tm = TM_PROJ
    widths = (2 * GMLP_W, SSD_W, SSD_XBC, MLA_Q_RANK, MLA_KV_RANK, LANES, LANES)
    dtypes = (BF16, BF16, BF16, BF16, BF16, F32, F32)
    return pl.pallas_call(
        _inproj_kernel,
        grid=(S // tm,),
        in_specs=[
            pl.BlockSpec((tm, D_MODEL), lambda i: (i, 0)),
            pl.BlockSpec((None, 1, D_MODEL), lambda i: (l, 0, 0)),
            pl.BlockSpec((None, D_MODEL, D_IN_PAD), lambda i: (l, 0, 0)),
        ],
        out_specs=[pl.BlockSpec((tm, w_), lambda i: (i, 0)) for w_ in widths],
        out_shape=[jax.ShapeDtypeStruct((S, w_), d_) for w_, d_ in zip(widths, dtypes)],
        compiler_params=pltpu.CompilerParams(
            dimension_semantics=("arbitrary",), vmem_limit_bytes=VMEM_LIMIT_V7X),
        name="inproj",
    )(x, g, w)


def _mixab_kernel(uv_ref, z_ref, xbc_ref, dt_ref, vg_ref, ws_ref, bsf_ref, cw_ref, cb_ref,
                  dtb_ref, alog_ref, dsk_ref, ng_ref, e2_ref, bd_ref,
                  ya_ref, yb_ref, state_ref, tail_ref, wm_ref):
    L = CHUNK
    row = lax.broadcasted_iota(jnp.int32, (L, L), 0)
    col = lax.broadcasted_iota(jnp.int32, (L, L), 1)
    causal = col <= row

    @pl.when(pl.program_id(0) == 0)
    def _():
        state_ref[...] = jnp.zeros_like(state_ref)
        tail_ref[...] = jnp.zeros_like(tail_ref)
        for h in range(GMLP_HEADS):
            wm_ref[h] = jnp.where(causal, ws_ref[h], 0.0).astype(BF16)

    uv = uv_ref[...].astype(F32)
    guv = 0.5 * uv * (1.0 + jnp.tanh(math.sqrt(2.0 / math.pi) * (uv + 0.044715 * (uv * uv * uv))))
    u = guv[:, :GMLP_W]
    vn = _rms(guv[:, GMLP_W:], vg_ref[...])
    head_a = _lane_group(GMLP_W, GMLP_HEAD_DIM)
    mixed = bsf_ref[...]
    for h in range(GMLP_HEADS):
        vm = jnp.where(head_a == h, vn, 0.0).astype(BF16)
        mixed = mixed + _dot(wm_ref[h], vm)
    ya_ref[...] = (u * mixed).astype(BF16)

    x = xbc_ref[...].astype(F32)
    xfull = jnp.concatenate([tail_ref[...], x], axis=0)
    conv = cb_ref[...] + cw_ref[SSD_CONV - 1:SSD_CONV, :] * x
    for k in range(SSD_CONV - 1):
        shifted = pltpu.roll(xfull, SSD_CONV - 1 - k, 0)[SUBLANES:SUBLANES + L]
        conv = conv + cw_ref[k:k + 1, :] * shifted
    tail_ref[...] = x[L - SUBLANES:L]
    xc = conv * _sigmoid(conv)
    xs = xc[:, :SSD_W]
    Bm = xc[:, SSD_W:SSD_W + SSD_BC]
    Cm = xc[:, SSD_W + SSD_BC:]

    dtr = dt_ref[...] + dtb_ref[...]
    dt = jnp.maximum(dtr, 0.0) + jnp.log1p(jnp.exp(-jnp.abs(dtr)))
    a_col = dt * -jnp.exp(alog_ref[...])
    for k in range(CHUNK.bit_length() - 1):
        a_col = a_col + jnp.where(row >= (1 << k), pltpu.roll(a_col, 1 << k, 0), 0.0)
    a_row = a_col.T
    dt_row = dt.T

    Bb = Bm.astype(BF16)
    Cb = Cm.astype(BF16)
    hpg = SSD_HEADS // SSD_GROUPS
    group_bc = _lane_group(SSD_BC, SSD_STATE)
    cb = [_dot_nt(jnp.where(group_bc == g, Cm, 0.0).astype(BF16), Bb) for g in range(SSD_GROUPS)]
    first_half = lax.broadcasted_iota(jnp.int32, (1, LANES), 1) < SSD_HEAD_DIM
    slabs = []
    for j in range(SSD_W // LANES):
        xsl = xs[:, j * LANES:(j + 1) * LANES]
        acc = None
        for hh in range(LANES // SSD_HEAD_DIM):
            h = j * (LANES // SSD_HEAD_DIM) + hh
            seg = a_col[:, h:h + 1] - a_row[h:h + 1, :]
            decay = jnp.exp(jnp.where(causal, seg, -jnp.inf))
            w = (cb[h // hpg] * decay * dt_row[h:h + 1, :]).astype(BF16)
            xm = jnp.where(first_half == (hh == 0), xsl, 0.0).astype(BF16)
            term = _dot(w, xm)
            acc = term if acc is None else acc + term
        slabs.append(acc)
    y = xs * dsk_ref[...] + jnp.concatenate(slabs, axis=1)

    def expand_heads(v):
        hi = v.astype(BF16)
        lo = (v - hi.astype(F32)).astype(BF16)
        return _dot(jnp.concatenate([hi, lo], axis=1), e2_ref[...])

    ea_w = expand_heads(jnp.exp(a_col))
    dte_w = expand_heads(jnp.exp(a_col[L - 1:L, :] - a_col) * dt)
    s_prev = state_ref[...]
    y = y + _dot(Cb, s_prev.astype(BF16)) * ea_w
    xsd = (xs * dte_w).astype(BF16)
    bt = Bm.T.astype(BF16)
    state_ref[...] = ea_w[L - 1:L, :] * s_prev + bd_ref[...] * _dot(bt, xsd)

    zf = z_ref[...].astype(F32)
    yg = y * (zf * _sigmoid(zf))
    yg2 = yg * yg
    group0 = lax.broadcasted_iota(jnp.int32, (1, SSD_W), 1) < SSD_W // SSD_GROUPS
    inv = [lax.rsqrt(jnp.sum(jnp.where(group0 == (g == 0), yg2, 0.0), axis=-1, keepdims=True)
                     * (1.0 / (SSD_W // SSD_GROUPS)) + EPS) for g in range(SSD_GROUPS)]
    yb_ref[...] = (yg * jnp.where(group0, inv[0], inv[1]) * ng_ref[...]).astype(BF16)


def _mixab(uv, z, xbc, dt, p, l):
    S = uv.shape[0]
    L = CHUNK

    def rows(w_):
        return pl.BlockSpec((L, w_), lambda i: (i, 0))

    def per_layer(*shape):
        return pl.BlockSpec((None,) + shape, lambda i: (l,) + (0,) * len(shape))

    def const(*shape):
        return pl.BlockSpec(shape, lambda i: (0,) * len(shape))

    return pl.pallas_call(
        _mixab_kernel,
        grid=(S // L,),
        in_specs=[
            rows(2 * GMLP_W), rows(SSD_W), rows(SSD_XBC), rows(LANES),
            per_layer(1, GMLP_W), per_layer(GMLP_HEADS, L, L), per_layer(L, GMLP_W),
            per_layer(SSD_CONV, SSD_XBC), per_layer(1, SSD_XBC),
            per_layer(1, LANES), per_layer(1, LANES), per_layer(1, SSD_W), per_layer(1, SSD_W),
            const(2 * LANES, SSD_W), const(SSD_BC, SSD_W),
        ],
        out_specs=[rows(GMLP_W), rows(SSD_W)],
        out_shape=[jax.ShapeDtypeStruct((S, GMLP_W), BF16), jax.ShapeDtypeStruct((S, SSD_W), BF16)],
        scratch_shapes=[pltpu.VMEM((SSD_BC, SSD_W), F32), pltpu.VMEM((SUBLANES, SSD_XBC), F32),
                        pltpu.VMEM((GMLP_HEADS, L, L), BF16)],
        compiler_params=pltpu.CompilerParams(
            dimension_semantics=("arbitrary",), vmem_limit_bytes=VMEM_LIMIT_V7X),
        name="mixab",
    )(uv, z, xbc, dt, p["vg"], p["ws"], p["bsf"], p["cw"], p["cb"], p["dtb"], p["alog"],
      p["dsk"], p["ng"], p["expand_heads2"], p["state_mask"])


def _mlaproj_kernel(cq_ref, ckv_ref, kr_ref, qg_ref, kvg_ref, wq_ref, wk_ref, wvt_ref,
                    ct_ref, s1_ref, s2_ref, q_out, k_out, vt_out):
    def rope(x, n):
        def wide(t):
            return t if n == 1 else jnp.concatenate([t] * n, axis=1)
        w = n * HEAD_SLAB
        up = pltpu.roll(x, w - MLA_ROPE // 2, 1)
        down = pltpu.roll(x, MLA_ROPE // 2, 1)
        return x * wide(ct_ref[...]) + up * wide(s1_ref[...]) + down * wide(s2_ref[...])

    qn = _rms(cq_ref[...].astype(F32), qg_ref[...]).astype(BF16)
    q = rope(_dot(qn, wq_ref[...]), MLA_HEADS) * (math.log2(math.e) / math.sqrt(MLA_NOPE + MLA_ROPE))
    q_out[...] = q.astype(BF16)

    kvn = _rms(ckv_ref[...].astype(F32), kvg_ref[...]).astype(BF16)
    kr = rope(kr_ref[...], 1)
    k = _dot(kvn, wk_ref[...]) + jnp.concatenate([kr] * MLA_HEADS, axis=1)
    k_out[...] = k.astype(BF16)
    srow = lax.broadcasted_iota(jnp.int32, (MLA_SLAB_W, 1), 0) & (HEAD_SLAB - 1)
    vt = _dot_nt(wvt_ref[...], kvn) + (srow == V_ONES_ROW).astype(F32)
    vt = vt.astype(BF16)
    for b in range(TM_PROJ // TK_ATT):
        vt_out[b] = vt[:, b * TK_ATT:(b + 1) * TK_ATT]


def _mlaproj(cq, ckv, kr, p, l, tabs):
    S = cq.shape[0]
    tm = TM_PROJ

    def rows(w_):
        return pl.BlockSpec((tm, w_), lambda i: (i, 0))

    def per_layer(*shape):
        return pl.BlockSpec((None,) + shape, lambda i: (l,) + (0,) * len(shape))

    out = jax.ShapeDtypeStruct((S, MLA_SLAB_W), BF16)
    nb = tm // TK_ATT
    return pl.pallas_call(
        _mlaproj_kernel,
        grid=(S // tm,),
        in_specs=[
            rows(MLA_Q_RANK), rows(MLA_KV_RANK), rows(LANES),
            per_layer(1, MLA_Q_RANK), per_layer(1, MLA_KV_RANK),
            per_layer(MLA_Q_RANK, MLA_SLAB_W), per_layer(MLA_KV_RANK, MLA_SLAB_W),
            per_layer(MLA_SLAB_W, MLA_KV_RANK),
            rows(LANES), rows(LANES), rows(LANES),
        ],
        out_specs=[rows(MLA_SLAB_W), rows(MLA_SLAB_W),
                   pl.BlockSpec((nb, MLA_SLAB_W, TK_ATT), lambda i: (i, 0, 0))],
        out_shape=[out, out, jax.ShapeDtypeStruct((S // TK_ATT, MLA_SLAB_W, TK_ATT), BF16)],
        compiler_params=pltpu.CompilerParams(
            dimension_semantics=("arbitrary",), vmem_limit_bytes=VMEM_LIMIT_V7X),
        name="mlaproj",
    )(cq, ckv, kr, p["qg"], p["kvg"], p["wq"], p["wk"], p["wvt"], *tabs)


def _flash_kernel(q_ref, k_ref, vt_ref, o_ref, sa_ref, sb_ref, m_ref, acc_ref):
    tq, tk = TQ_ATT, TK_ATT
    n_diag = tq // tk
    qi = pl.program_id(1)
    heads = (0, 1)
    bufs = (sa_ref, sb_ref)

    def slab(hh):
        return slice(hh * HEAD_SLAB, (hh + 1) * HEAD_SLAB)

    def scores(blk, s_ref, c0=0):
        start = pl.multiple_of(blk * tk, tk)
        for hh in heads:
            s_ref[hh, :, c0:] = _dot_nt(k_ref[pl.ds(start, tk), slab(hh)], q_ref[c0:, slab(hh)])

    def softmax_pv(blk, s_ref, c0=0, masked=False):
        for hh in heads:
            s = s_ref[hh, :, c0:]
            if masked:
                key = lax.broadcasted_iota(jnp.int32, s.shape, 0)
                qry = lax.broadcasted_iota(jnp.int32, s.shape, 1)
                s = jnp.where(key <= qry, s, -jnp.inf)
            m_old = m_ref[hh, :, c0:]
            m_new = jnp.maximum(m_old, jnp.max(s, axis=0, keepdims=True))
            alpha = jnp.exp2(m_old - m_new)
            p = jnp.exp2(s - m_new).astype(BF16)
            acc_ref[hh, :, c0:] = alpha * acc_ref[hh, :, c0:] + _dot(vt_ref[blk, slab(hh), :], p)
            m_ref[hh, :, c0:] = m_new

    m_ref[...] = jnp.full_like(m_ref, -jnp.inf)
    acc_ref[...] = jnp.zeros_like(acc_ref)
    scores(0, sa_ref)

    def body(j, carry):
        scores(2 * j + 1, sb_ref)
        softmax_pv(2 * j, sa_ref)
        scores(2 * j + 2, sa_ref)
        softmax_pv(2 * j + 1, sb_ref)
        return carry

    first_diag = qi * n_diag
    lax.fori_loop(0, first_diag // 2, body, 0)

    for d in range(n_diag):
        if d + 1 < n_diag:
            scores(first_diag + d + 1, bufs[(d + 1) % 2], (d + 1) * tk)
        softmax_pv(first_diag + d, bufs[d % 2], d * tk, masked=True)

    outs = []
    for hh in heads:
        acc = acc_ref[hh]
        outs.append((acc * (1.0 / acc[V_ONES_ROW:V_ONES_ROW + 1, :])).T)
    lane = lax.broadcasted_iota(jnp.int32, (1, HEAD_SLAB), 1)
    o_ref[...] = jnp.where(lane < MLA_V, outs[0], pltpu.roll(outs[1], MLA_V, 1)).astype(BF16)


def _flash(q, k, vt):
    S = q.shape[0]
    tq, tk = TQ_ATT, TK_ATT
    pair_w = 2 * HEAD_SLAB
    return pl.pallas_call(
        _flash_kernel,
        grid=(MLA_HEADS // 2, S // tq),
        in_specs=[
            pl.BlockSpec((tq, pair_w), lambda pr, i: (i, pr)),
            pl.BlockSpec((S, pair_w), lambda pr, i: (0, pr)),
            pl.BlockSpec((S // tk, pair_w, tk), lambda pr, i: (0, pr, 0)),
        ],
        out_specs=pl.BlockSpec((tq, 2 * MLA_V), lambda pr, i: (i, pr)),
        out_shape=jax.ShapeDtypeStruct((S, MLA_W), BF16),
        scratch_shapes=[
            pltpu.VMEM((2, tk, tq), F32), pltpu.VMEM((2, tk, tq), F32),
            pltpu.VMEM((2, 1, tq), F32), pltpu.VMEM((2, HEAD_SLAB, tq), F32),
        ],
        compiler_params=pltpu.CompilerParams(
            dimension_semantics=("arbitrary", "arbitrary"), vmem_limit_bytes=VMEM_LIMIT_V7X),
        name="flash",
    )(q, k, vt)


def _outmlp_kernel(x_ref, ya_ref, yb_ref, yc_ref, wo_ref, g_ref, w1_ref, w2_ref, fg_ref, o_ref,
                   hn_ref, *, final):
    c = pl.program_id(1)

    @pl.when(c == 0)
    def _():
        mix = jnp.concatenate([ya_ref[...], yb_ref[...], yc_ref[...]], axis=1)
        x = x_ref[...] + _dot(mix, wo_ref[...])
        o_ref[...] = x
        hn_ref[...] = _rms(x, g_ref[...]).astype(BF16)

    h1 = _dot(hn_ref[...], w1_ref[...])
    h1 = jnp.square(jnp.maximum(h1, 0.0)).astype(BF16)
    o_ref[...] += _dot(h1, w2_ref[...])

    if final:
        @pl.when(c == pl.num_programs(1) - 1)
        def _():
            o_ref[...] = _rms(o_ref[...], fg_ref[...])


def _outmlp(x, ya, yb, yc, p, l, final_g, final):
    S = x.shape[0]
    tm = TM_MLP

    def rows(w_):
        return pl.BlockSpec((tm, w_), lambda i, c: (i, 0))

    def per_layer(*shape):
        return pl.BlockSpec((None,) + shape, lambda i, c: (l,) + (0,) * len(shape))

    return pl.pallas_call(
        functools.partial(_outmlp_kernel, final=final),
        grid=(S // tm, D_FF // FF_CHUNK),
        in_specs=[
            rows(D_MODEL), rows(GMLP_W), rows(SSD_W), rows(MLA_W),
            per_layer(D_MODEL, D_MODEL), per_layer(1, D_MODEL),
            pl.BlockSpec((None, D_MODEL, FF_CHUNK), lambda i, c: (l, 0, c)),
            pl.BlockSpec((None, FF_CHUNK, D_MODEL), lambda i, c: (l, c, 0)),
            pl.BlockSpec((1, D_MODEL), lambda i, c: (0, 0)),
        ],
        out_specs=rows(D_MODEL),
        out_shape=jax.ShapeDtypeStruct((S, D_MODEL), F32),
        scratch_shapes=[pltpu.VMEM((tm, D_MODEL), BF16)],
        compiler_params=pltpu.CompilerParams(
            dimension_semantics=("arbitrary", "arbitrary"), vmem_limit_bytes=VMEM_LIMIT_V7X),
        name="outmlp",
    )(x, ya, yb, yc, p["wo"], p["mlp_g"], p["w1"], p["w2"], final_g)


def _pad_lanes(a, n):
    return jnp.pad(a, [(0, 0)] * (a.ndim - 1) + [(0, n - a.shape[-1])])


def _head_slabs(w, heads, width, lane0):
    lead = w.shape[:-1]
    w = w.reshape(lead + (heads, width))
    w = jnp.pad(w, [(0, 0)] * len(lead) + [(0, 0), (lane0, HEAD_SLAB - lane0 - width)])
    return w.reshape(lead + (heads * HEAD_SLAB,))


def _layout_params(norm_mix_g, w_in, gmlp_v_norm_g, gmlp_w_s, gmlp_b_s, ssd_conv_w, ssd_conv_b,
                   ssd_dt_bias, ssd_a_log, ssd_d, ssd_norm_g, mla_q_norm_g, mla_w_qb,
                   mla_kv_norm_g, mla_w_kvb, w_out, norm_mlp_g, mlp_w1, mlp_w2):
    widths = (2 * GMLP_W, SSD_W, SSD_XBC, SSD_HEADS, MLA_Q_RANK, MLA_KV_RANK, MLA_ROPE)
    offs = np.cumsum((0,) + widths)
    uv, z, xbc, dt, cq, ckv, kr = [w_in[..., offs[i]:offs[i + 1]] for i in range(7)]
    kr = jnp.pad(kr, [(0, 0), (0, 0), (ROPE_LANE0, HEAD_SLAB - ROPE_LANE0 - MLA_ROPE)])
    w_in_p = jnp.concatenate([uv, z, xbc, cq, ckv, _pad_lanes(dt, LANES), kr], axis=-1).astype(BF16)

    qk_w = MLA_NOPE + MLA_ROPE
    wq = _head_slabs(mla_w_qb, MLA_HEADS, qk_w, 0)
    kv = mla_w_kvb.reshape(DEPTH, MLA_KV_RANK, MLA_HEADS, MLA_NOPE + MLA_V)
    wk = _head_slabs(kv[..., :MLA_NOPE].reshape(DEPTH, MLA_KV_RANK, -1), MLA_HEADS, MLA_NOPE, 0)
    wv = _head_slabs(kv[..., MLA_NOPE:].reshape(DEPTH, MLA_KV_RANK, -1), MLA_HEADS, MLA_V, 0)

    return {
        "mix_g": norm_mix_g[:, None, :],
        "w_in": w_in_p,
        "vg": gmlp_v_norm_g[:, None, :],
        "ws": gmlp_w_s,
        "bsf": jnp.repeat(jnp.swapaxes(gmlp_b_s, 1, 2), GMLP_HEAD_DIM, axis=-1),
        "cw": ssd_conv_w,
        "cb": ssd_conv_b[:, None, :],
        "dtb": _pad_lanes(ssd_dt_bias, LANES)[:, None, :],
        "alog": _pad_lanes(ssd_a_log, LANES)[:, None, :],
        "dsk": jnp.repeat(ssd_d, SSD_HEAD_DIM, axis=-1)[:, None, :],
        "ng": ssd_norm_g[:, None, :],
        "qg": mla_q_norm_g[:, None, :],
        "kvg": mla_kv_norm_g[:, None, :],
        "wq": wq.astype(BF16),
        "wk": wk.astype(BF16),
        "wvt": jnp.swapaxes(wv, 1, 2).astype(BF16),
        "wo": w_out.astype(BF16),
        "mlp_g": norm_mlp_g[:, None, :],
        "w1": mlp_w1.astype(BF16),
        "w2": mlp_w2.astype(BF16),
    }


def _constant_tables():
    heads = np.arange(SSD_W) // SSD_HEAD_DIM
    expand_heads = (np.arange(LANES)[:, None] == heads[None, :]).astype(np.float32)
    groups = np.arange(SSD_W) // (SSD_W // SSD_GROUPS)
    state_mask = ((np.arange(SSD_BC) // SSD_STATE)[:, None] == groups[None, :]).astype(np.float32)
    return {
        "expand_heads2": jnp.asarray(np.concatenate([expand_heads, expand_heads], axis=0), dtype=BF16),
        "state_mask": jnp.asarray(state_mask),
    }


def _rope_tables(S):
    half = MLA_ROPE // 2
    pos = jnp.arange(S, dtype=F32)
    inv_freq = jnp.power(ROPE_BASE, -jnp.arange(half, dtype=F32) / half)
    ang = pos[:, None] * inv_freq[None, :]
    cos, sin = jnp.cos(ang), jnp.sin(ang)
    zeros = jnp.zeros((S, half), F32)
    pad = jnp.zeros((S, HEAD_SLAB - ROPE_LANE0 - MLA_ROPE), F32)
    ct = jnp.concatenate([jnp.ones((S, ROPE_LANE0), F32), cos, cos, pad], axis=1)
    s1 = jnp.concatenate([jnp.zeros((S, ROPE_LANE0), F32), -sin, zeros, pad], axis=1)
    s2 = jnp.concatenate([jnp.zeros((S, ROPE_LANE0), F32), zeros, sin, pad], axis=1)
    return ct, s1, s2


def kernel(x, norm_mix_g, w_in, gmlp_v_norm_g, gmlp_w_s, gmlp_b_s, ssd_conv_w, ssd_conv_b, ssd_dt_bias, ssd_a_log, ssd_d, ssd_norm_g, mla_q_norm_g, mla_w_qb, mla_kv_norm_g, mla_w_kvb, w_out, norm_mlp_g, mlp_w1, mlp_w2, final_norm_g):
    B, S, _ = x.shape
    assert B == 1 and S % TQ_ATT == 0 and S % TM_PROJ == 0 and S % TM_MLP == 0
    p = _layout_params(norm_mix_g, w_in, gmlp_v_norm_g, gmlp_w_s, gmlp_b_s, ssd_conv_w, ssd_conv_b,
                       ssd_dt_bias, ssd_a_log, ssd_d, ssd_norm_g, mla_q_norm_g, mla_w_qb,
                       mla_kv_norm_g, mla_w_kvb, w_out, norm_mlp_g, mlp_w1, mlp_w2)
    p.update(_constant_tables())
    tabs = _rope_tables(S)
    fg = final_norm_g[None, :]
    xs = x.reshape(S, D_MODEL)
    for l in range(DEPTH):
        uv, z, xbc, cq, ckv, dt, kr = _inproj(xs, p["mix_g"], p["w_in"], l)
        ya, yb = _mixab(uv, z, xbc, dt, p, l)
        q, k, vt = _mlaproj(cq, ckv, kr, p, l, tabs)
        yc = _flash(q, k, vt)
        xs = _outmlp(xs, ya, yb, yc, p, l, fg, final=(l == DEPTH - 1))
    return xs.reshape(B, S, D_MODEL)
```

```python
import functools
import math

import numpy as np
import jax
import jax.numpy as jnp
from jax import lax
from jax.experimental import pallas as pl
from jax.experimental.pallas import tpu as pltpu

F32 = jnp.float32
BF16 = jnp.bfloat16
HIGHEST = lax.Precision.HIGHEST

D_MODEL = 1024
DEPTH = 4
CHUNK = 128
GMLP_HEADS = 4
GMLP_HEAD_DIM = 64
GMLP_W = GMLP_HEADS * GMLP_HEAD_DIM
SSD_HEADS = 6
SSD_HEAD_DIM = 64
SSD_W = SSD_HEADS * SSD_HEAD_DIM
SSD_GROUPS = 2
SSD_STATE = 64
SSD_CONV = 4
SSD_BC = SSD_GROUPS * SSD_STATE
SSD_XBC = SSD_W + 2 * SSD_BC
MLA_HEADS = 6
MLA_NOPE = 64
MLA_ROPE = 32
MLA_V = 64
MLA_W = MLA_HEADS * MLA_V
MLA_Q_RANK = 384
MLA_KV_RANK = 256
ROPE_BASE = 10000.0
D_FF = 4 * D_MODEL
EPS = 1e-6

LANES = 128
SUBLANES = 8
HEAD_SLAB = LANES
MLA_SLAB_W = MLA_HEADS * HEAD_SLAB
V_ONES_ROW = MLA_V
ROPE_LANE0 = MLA_NOPE
VMEM_LIMIT_V7X = 56 * 1024 * 1024

_OFF_UV = 0
_OFF_Z = _OFF_UV + 2 * GMLP_W
_OFF_XBC = _OFF_Z + SSD_W
_OFF_CQ = _OFF_XBC + SSD_XBC
_OFF_CKV = _OFF_CQ + MLA_Q_RANK
_OFF_DT = _OFF_CKV + MLA_KV_RANK
_OFF_KR = _OFF_DT + LANES
D_IN_PAD = _OFF_KR + LANES

TM_PROJ = 1024
TM_MLP = 1024
TQ_ATT = 1024
TK_ATT = 512
FF_CHUNK = 1024


def _rms(x, g):
    var = jnp.mean(x * x, axis=-1, keepdims=True)
    return (x * lax.rsqrt(var + EPS)) * g


def _sigmoid(x):
    return 1.0 / (1.0 + jnp.exp(-x))


def _lane_group(width, group_width):
    shift = group_width.bit_length() - 1
    assert 1 << shift == group_width
    return lax.shift_right_logical(lax.broadcasted_iota(jnp.int32, (1, width), 1), shift)


def _dot(a, b, precision=None):
    return jnp.dot(a, b, preferred_element_type=F32, precision=precision)


def _dot_nt(a, b):
    return lax.dot_general(a, b, (((1,), (1,)), ((), ())), preferred_element_type=F32)


def _inproj_kernel(x_ref, g_ref, w_ref, uv_ref, z_ref, xbc_ref, cq_ref, ckv_ref, dt_ref, kr_ref):
    h = _rms(x_ref[...], g_ref[...]).astype(BF16)

    def seg(a, b):
        return _dot(h, w_ref[:, a:b])

    uv_ref[...] = seg(_OFF_UV, _OFF_Z).astype(BF16)
    z_ref[...] = seg(_OFF_Z, _OFF_XBC).astype(BF16)
    xbc_ref[...] = seg(_OFF_XBC, _OFF_CQ).astype(BF16)
    cq_ref[...] = seg(_OFF_CQ, _OFF_CKV).astype(BF16)
    ckv_ref[...] = seg(_OFF_CKV, _OFF_DT).astype(BF16)
    dt_ref[...] = seg(_OFF_DT, _OFF_KR)
    kr_ref[...] = seg(_OFF_KR, D_IN_PAD)


def _inproj(x, g, w, l):
    S = x.shape[0]
    tm = TM_PROJ
    widths = (2 * GMLP_W, SSD_W, SSD_XBC, MLA_Q_RANK, MLA_KV_RANK, LANES, LANES)
    dtypes = (BF16, BF16, BF16, BF16, BF16, F32, F32)
    return pl.pallas_call(
        _inproj_kernel,
        grid=(S // tm,),
        in_specs=[
            pl.BlockSpec((tm, D_MODEL), lambda i: (i, 0)),
            pl.BlockSpec((None, 1, D_MODEL), lambda i: (l, 0, 0)),
            pl.BlockSpec((None, D_MODEL, D_IN_PAD), lambda i: (l, 0, 0)),
        ],
        out_specs=[pl.BlockSpec((tm, w_), lambda i: (i, 0)) for w_ in widths],
        out_shape=[jax.ShapeDtypeStruct((S, w_), d_) for w_, d_ in zip(widths, dtypes)],
        compiler_params=pltpu.CompilerParams(
            dimension_semantics=("arbitrary",), vmem_limit_bytes=VMEM_LIMIT_V7X),
        name="inproj",
    )(x, g, w)


def _mixab_kernel(uv_ref, z_ref, xbc_ref, dt_ref, vg_ref, ws_ref, bsf_ref, cw_ref, cb_ref,
                  dtb_ref, alog_ref, dsk_ref, ng_ref, e2_ref, bd_ref,
                  ya_ref, yb_ref, state_ref, tail_ref, wm_ref):
    L = CHUNK
    row = lax.broadcasted_iota(jnp.int32, (L, L), 0)
    col = lax.broadcasted_iota(jnp.int32, (L, L), 1)
    causal = col <= row

    @pl.when(pl.program_id(0) == 0)
    def _():
        state_ref[...] = jnp.zeros_like(state_ref)
        tail_ref[...] = jnp.zeros_like(tail_ref)
        for h in range(GMLP_HEADS):
            wm_ref[h] = jnp.where(causal, ws_ref[h], 0.0).astype(BF16)

    uv = uv_ref[...].astype(F32)
    guv = 0.5 * uv * (1.0 + jnp.tanh(math.sqrt(2.0 / math.pi) * (uv + 0.044715 * (uv * uv * uv))))
    u = guv[:, :GMLP_W]
    vn = _rms(guv[:, GMLP_W:], vg_ref[...])
    head_a = _lane_group(GMLP_W, GMLP_HEAD_DIM)
    mixed = bsf_ref[...]
    for h in range(GMLP_HEADS):
        vm = jnp.where(head_a == h, vn, 0.0).astype(BF16)
        mixed = mixed + _dot(wm_ref[h], vm)
    ya_ref[...] = (u * mixed).astype(BF16)

    x = xbc_ref[...].astype(F32)
    xfull = jnp.concatenate([tail_ref[...], x], axis=0)
    conv = cb_ref[...] + cw_ref[SSD_CONV - 1:SSD_CONV, :] * x
    for k in range(SSD_CONV - 1):
        shifted = pltpu.roll(xfull, SSD_CONV - 1 - k, 0)[SUBLANES:SUBLANES + L]
        conv = conv + cw_ref[k:k + 1, :] * shifted
    tail_ref[...] = x[L - SUBLANES:L]
    xc = conv * _sigmoid(conv)
    xs = xc[:, :SSD_W]
    Bm = xc[:, SSD_W:SSD_W + SSD_BC]
    Cm = xc[:, SSD_W + SSD_BC:]

    dtr = dt_ref[...] + dtb_ref[...]
    dt = jnp.maximum(dtr, 0.0) + jnp.log1p(jnp.exp(-jnp.abs(dtr)))
    a_col = dt * -jnp.exp(alog_ref[...])
    for k in range(CHUNK.bit_length() - 1):
        a_col = a_col + jnp.where(row >= (1 << k), pltpu.roll(a_col, 1 << k, 0), 0.0)
    a_row = a_col.T
    dt_row = dt.T

    Bb = Bm.astype(BF16)
    Cb = Cm.astype(BF16)
    hpg = SSD_HEADS // SSD_GROUPS
    group_bc = _lane_group(SSD_BC, SSD_STATE)
    cb = [_dot_nt(jnp.where(group_bc == g, Cm, 0.0).astype(BF16), Bb) for g in range(SSD_GROUPS)]
    first_half = lax.broadcasted_iota(jnp.int32, (1, LANES), 1) < SSD_HEAD_DIM
    slabs = []
    for j in range(SSD_W // LANES):
        xsl = xs[:, j * LANES:(j + 1) * LANES]
        acc = None
        for hh in range(LANES // SSD_HEAD_DIM):
            h = j * (LANES // SSD_HEAD_DIM) + hh
            seg = a_col[:, h:h + 1] - a_row[h:h + 1, :]
            decay = jnp.exp(jnp.where(causal, seg, -jnp.inf))
            w = (cb[h // hpg] * decay * dt_row[h:h + 1, :]).astype(BF16)
            xm = jnp.where(first_half == (hh == 0), xsl, 0.0).astype(BF16)
            term = _dot(w, xm)
            acc = term if acc is None else acc + term
        slabs.append(acc)
    y = xs * dsk_ref[...] + jnp.concatenate(slabs, axis=1)

    def expand_heads(v):
        hi = v.astype(BF16)
        lo = (v - hi.astype(F32)).astype(BF16)
        return _dot(jnp.concatenate([hi, lo], axis=1), e2_ref[...])

    ea_w = expand_heads(jnp.exp(a_col))
    dte_w = expand_heads(jnp.exp(a_col[L - 1:L, :] - a_col) * dt)
    s_prev = state_ref[...]
    y = y + _dot(Cb, s_prev.astype(BF16)) * ea_w
    xsd = (xs * dte_w).astype(BF16)
    bt = Bm.T.astype(BF16)
    state_ref[...] = ea_w[L - 1:L, :] * s_prev + bd_ref[...] * _dot(bt, xsd)

    zf = z_ref[...].astype(F32)
    yg = y * (zf * _sigmoid(zf))
    yg2 = yg * yg
    group0 = lax.broadcasted_iota(jnp.int32, (1, SSD_W), 1) < SSD_W // SSD_GROUPS
    inv = [lax.rsqrt(jnp.sum(jnp.where(group0 == (g == 0), yg2, 0.0), axis=-1, keepdims=True)
                     * (1.0 / (SSD_W // SSD_GROUPS)) + EPS) for g in range(SSD_GROUPS)]
    yb_ref[...] = (yg * jnp.where(group0, inv[0], inv[1]) * ng_ref[...]).astype(BF16)


def _mixab(uv, z, xbc, dt, p, l):
    S = uv.shape[0]
    L = CHUNK

    def rows(w_):
        return pl.BlockSpec((L, w_), lambda i: (i, 0))

    def per_layer(*shape):
        return pl.BlockSpec((None,) + shape, lambda i: (l,) + (0,) * len(shape))

    def const(*shape):
        return pl.BlockSpec(shape, lambda i: (0,) * len(shape))

    return pl.pallas_call(
        _mixab_kernel,
        grid=(S // L,),
        in_specs=[
            rows(2 * GMLP_W), rows(SSD_W), rows(SSD_XBC), rows(LANES),
            per_layer(1, GMLP_W), per_layer(GMLP_HEADS, L, L), per_layer(L, GMLP_W),
            per_layer(SSD_CONV, SSD_XBC), per_layer(1, SSD_XBC),
            per_layer(1, LANES), per_layer(1, LANES), per_layer(1, SSD_W), per_layer(1, SSD_W),
            const(2 * LANES, SSD_W), const(SSD_BC, SSD_W),
        ],
        out_specs=[rows(GMLP_W), rows(SSD_W)],
        out_shape=[jax.ShapeDtypeStruct((S, GMLP_W), BF16), jax.ShapeDtypeStruct((S, SSD_W), BF16)],
        scratch_shapes=[pltpu.VMEM((SSD_BC, SSD_W), F32), pltpu.VMEM((SUBLANES, SSD_XBC), F32),
                        pltpu.VMEM((GMLP_HEADS, L, L), BF16)],
        compiler_params=pltpu.CompilerParams(
            dimension_semantics=("arbitrary",), vmem_limit_bytes=VMEM_LIMIT_V7X),
        name="mixab",
    )(uv, z, xbc, dt, p["vg"], p["ws"], p["bsf"], p["cw"], p["cb"], p["dtb"], p["alog"],
      p["dsk"], p["ng"], p["expand_heads2"], p["state_mask"])


def _mlaproj_kernel(cq_ref, ckv_ref, kr_ref, qg_ref, kvg_ref, wq_ref, wk_ref, wvt_ref,
                    ct_ref, s1_ref, s2_ref, q_out, k_out, vt_out):
    def rope(x, n):
        def wide(t):
            return t if n == 1 else jnp.concatenate([t] * n, axis=1)
        w = n * HEAD_SLAB
        up = pltpu.roll(x, w - MLA_ROPE // 2, 1)
        down = pltpu.roll(x, MLA_ROPE // 2, 1)
        return x * wide(ct_ref[...]) + up * wide(s1_ref[...]) + down * wide(s2_ref[...])

    qn = _rms(cq_ref[...].astype(F32), qg_ref[...]).astype(BF16)
    q = rope(_dot(qn, wq_ref[...]), MLA_HEADS) * (math.log2(math.e) / math.sqrt(MLA_NOPE + MLA_ROPE))
    q_out[...] = q.astype(BF16)

    kvn = _rms(ckv_ref[...].astype(F32), kvg_ref[...]).astype(BF16)
    kr = rope(kr_ref[...], 1)
    k = _dot(kvn, wk_ref[...]) + jnp.concatenate([kr] * MLA_HEADS, axis=1)
    k_out[...] = k.astype(BF16)
    srow = lax.broadcasted_iota(jnp.int32, (MLA_SLAB_W, 1), 0) & (HEAD_SLAB - 1)
    vt = _dot_nt(wvt_ref[...], kvn) + (srow == V_ONES_ROW).astype(F32)
    vt = vt.astype(BF16)
    for b in range(TM_PROJ // TK_ATT):
        vt_out[b] = vt[:, b * TK_ATT:(b + 1) * TK_ATT]


def _mlaproj(cq, ckv, kr, p, l, tabs):
    S = cq.shape[0]
    tm = TM_PROJ

    def rows(w_):
        return pl.BlockSpec((tm, w_), lambda i: (i, 0))

    def per_layer(*shape):
        return pl.BlockSpec((None,) + shape, lambda i: (l,) + (0,) * len(shape))

    out = jax.ShapeDtypeStruct((S, MLA_SLAB_W), BF16)
    nb = tm // TK_ATT
    return pl.pallas_call(
        _mlaproj_kernel,
        grid=(S // tm,),
        in_specs=[
            rows(MLA_Q_RANK), rows(MLA_KV_RANK), rows(LANES),
            per_layer(1, MLA_Q_RANK), per_layer(1, MLA_KV_RANK),
            per_layer(MLA_Q_RANK, MLA_SLAB_W), per_layer(MLA_KV_RANK, MLA_SLAB_W),
            per_layer(MLA_SLAB_W, MLA_KV_RANK),
            rows(LANES), rows(LANES), rows(LANES),
        ],
        out_specs=[rows(MLA_SLAB_W), rows(MLA_SLAB_W),
                   pl.BlockSpec((nb, MLA_SLAB_W, TK_ATT), lambda i: (i, 0, 0))],
        out_shape=[out, out, jax.ShapeDtypeStruct((S // TK_ATT, MLA_SLAB_W, TK_ATT), BF16)],
        compiler_params=pltpu.CompilerParams(
            dimension_semantics=("arbitrary",), vmem_limit_bytes=VMEM_LIMIT_V7X),
        name="mlaproj",
    )(cq, ckv, kr, p["qg"], p["kvg"], p["wq"], p["wk"], p["wvt"], *tabs)


def _flash_kernel(q_ref, k_ref, vt_ref, o_ref, sa_ref, sb_ref, m_ref, acc_ref):
    tq, tk = TQ_ATT, TK_ATT
    n_diag = tq // tk
    qi = pl.program_id(1)
    heads = (0, 1)
    bufs = (sa_ref, sb_ref)

    def slab(hh):
        return slice(hh * HEAD_SLAB, (hh + 1) * HEAD_SLAB)

    def scores(blk, s_ref, c0=0):
        start = pl.multiple_of(blk * tk, tk)
        for hh in heads:
            s_ref[hh, :, c0:] = _dot_nt(k_ref[pl.ds(start, tk), slab(hh)], q_ref[c0:, slab(hh)])

    def softmax_pv(blk, s_ref, c0=0, masked=False):
        for hh in heads:
            s = s_ref[hh, :, c0:]
            if masked:
                key = lax.broadcasted_iota(jnp.int32, s.shape, 0)
                qry = lax.broadcasted_iota(jnp.int32, s.shape, 1)
                s = jnp.where(key <= qry, s, -jnp.inf)
            m_old = m_ref[hh, :, c0:]
            m_new = jnp.maximum(m_old, jnp.max(s, axis=0, keepdims=True))
            alpha = jnp.exp2(m_old - m_new)
            p = jnp.exp2(s - m_new).astype(BF16)
            acc_ref[hh, :, c0:] = alpha * acc_ref[hh, :, c0:] + _dot(vt_ref[blk, slab(hh), :], p)
            m_ref[hh, :, c0:] = m_new

    m_ref[...] = jnp.full_like(m_ref, -jnp.inf)
    acc_ref[...] = jnp.zeros_like(acc_ref)
    scores(0, sa_ref)

    def pair(b):
        scores(b + 1, sb_ref)
        softmax_pv(b, sa_ref)
        scores(b + 2, sa_ref)
        softmax_pv(b + 1, sb_ref)

    def body(j, carry):
        pair(4 * j)
        pair(4 * j + 2)
        return carry

    first_diag = qi * n_diag
    n_pairs = first_diag // 2
    lax.fori_loop(0, n_pairs // 2, body, 0)

    @pl.when(n_pairs % 2 == 1)
    def _():
        pair(first_diag - 2)

    for d in range(n_diag):
        if d + 1 < n_diag:
            scores(first_diag + d + 1, bufs[(d + 1) % 2], (d + 1) * tk)
        softmax_pv(first_diag + d, bufs[d % 2], d * tk, masked=True)

    outs = []
    for hh in heads:
        acc = acc_ref[hh]
        outs.append((acc * (1.0 / acc[V_ONES_ROW:V_ONES_ROW + 1, :])).T)
    lane = lax.broadcasted_iota(jnp.int32, (1, HEAD_SLAB), 1)
    o_ref[...] = jnp.where(lane < MLA_V, outs[0], pltpu.roll(outs[1], MLA_V, 1)).astype(BF16)


def _flash(q, k, vt):
    S = q.shape[0]
    tq, tk = TQ_ATT, TK_ATT
    pair_w = 2 * HEAD_SLAB
    return pl.pallas_call(
        _flash_kernel,
        grid=(MLA_HEADS // 2, S // tq),
        in_specs=[
            pl.BlockSpec((tq, pair_w), lambda pr, i: (i, pr)),
            pl.BlockSpec((S, pair_w), lambda pr, i: (0, pr)),
            pl.BlockSpec((S // tk, pair_w, tk), lambda pr, i: (0, pr, 0)),
        ],
        out_specs=pl.BlockSpec((tq, 2 * MLA_V), lambda pr, i: (i, pr)),
        out_shape=jax.ShapeDtypeStruct((S, MLA_W), BF16),
        scratch_shapes=[
            pltpu.VMEM((2, tk, tq), F32), pltpu.VMEM((2, tk, tq), F32),
            pltpu.VMEM((2, 1, tq), F32), pltpu.VMEM((2, HEAD_SLAB, tq), F32),
        ],
        compiler_params=pltpu.CompilerParams(
            dimension_semantics=("arbitrary", "arbitrary"), vmem_limit_bytes=VMEM_LIMIT_V7X),
        name="flash",
    )(q, k, vt)


def _outmlp_kernel(x_ref, ya_ref, yb_ref, yc_ref, wo_ref, g_ref, w1_ref, w2_ref, fg_ref, o_ref,
                   hn_ref, *, final):
    c = pl.program_id(1)

    @pl.when(c == 0)
    def _():
        mix = jnp.concatenate([ya_ref[...], yb_ref[...], yc_ref[...]], axis=1)
        x = x_ref[...] + _dot(mix, wo_ref[...])
        o_ref[...] = x
        hn_ref[...] = _rms(x, g_ref[...]).astype(BF16)

    h1 = _dot(hn_ref[...], w1_ref[...])
    h1 = jnp.square(jnp.maximum(h1, 0.0)).astype(BF16)
    o_ref[...] += _dot(h1, w2_ref[...])

    if final:
        @pl.when(c == pl.num_programs(1) - 1)
        def _():
            o_ref[...] = _rms(o_ref[...], fg_ref[...])


def _outmlp(x, ya, yb, yc, p, l, final_g, final):
    S = x.shape[0]
    tm = TM_MLP

    def rows(w_):
        return pl.BlockSpec((tm, w_), lambda i, c: (i, 0))

    def per_layer(*shape):
        return pl.BlockSpec((None,) + shape, lambda i, c: (l,) + (0,) * len(shape))

    return pl.pallas_call(
        functools.partial(_outmlp_kernel, final=final),
        grid=(S // tm, D_FF // FF_CHUNK),
        in_specs=[
            rows(D_MODEL), rows(GMLP_W), rows(SSD_W), rows(MLA_W),
            per_layer(D_MODEL, D_MODEL), per_layer(1, D_MODEL),
            pl.BlockSpec((None, D_MODEL, FF_CHUNK), lambda i, c: (l, 0, c)),
            pl.BlockSpec((None, FF_CHUNK, D_MODEL), lambda i, c: (l, c, 0)),
            pl.BlockSpec((1, D_MODEL), lambda i, c: (0, 0)),
        ],
        out_specs=rows(D_MODEL),
        out_shape=jax.ShapeDtypeStruct((S, D_MODEL), F32),
        scratch_shapes=[pltpu.VMEM((tm, D_MODEL), BF16)],
        compiler_params=pltpu.CompilerParams(
            dimension_semantics=("arbitrary", "arbitrary"), vmem_limit_bytes=VMEM_LIMIT_V7X),
        name="outmlp",
    )(x, ya, yb, yc, p["wo"], p["mlp_g"], p["w1"], p["w2"], final_g)


def _pad_lanes(a, n):
    return jnp.pad(a, [(0, 0)] * (a.ndim - 1) + [(0, n - a.shape[-1])])


def _head_slabs(w, heads, width, lane0):
    lead = w.shape[:-1]
    w = w.reshape(lead + (heads, width))
    w = jnp.pad(w, [(0, 0)] * len(lead) + [(0, 0), (lane0, HEAD_SLAB - lane0 - width)])
    return w.reshape(lead + (heads * HEAD_SLAB,))


def _layout_params(norm_mix_g, w_in, gmlp_v_norm_g, gmlp_w_s, gmlp_b_s, ssd_conv_w, ssd_conv_b,
                   ssd_dt_bias, ssd_a_log, ssd_d, ssd_norm_g, mla_q_norm_g, mla_w_qb,
                   mla_kv_norm_g, mla_w_kvb, w_out, norm_mlp_g, mlp_w1, mlp_w2):
    widths = (2 * GMLP_W, SSD_W, SSD_XBC, SSD_HEADS, MLA_Q_RANK, MLA_KV_RANK, MLA_ROPE)
    offs = np.cumsum((0,) + widths)
    uv, z, xbc, dt, cq, ckv, kr = [w_in[..., offs[i]:offs[i + 1]] for i in range(7)]
    kr = jnp.pad(kr, [(0, 0), (0, 0), (ROPE_LANE0, HEAD_SLAB - ROPE_LANE0 - MLA_ROPE)])
    w_in_p = jnp.concatenate([uv, z, xbc, cq, ckv, _pad_lanes(dt, LANES), kr], axis=-1).astype(BF16)

    qk_w = MLA_NOPE + MLA_ROPE
    wq = _head_slabs(mla_w_qb, MLA_HEADS, qk_w, 0)
    kv = mla_w_kvb.reshape(DEPTH, MLA_KV_RANK, MLA_HEADS, MLA_NOPE + MLA_V)
    wk = _head_slabs(kv[..., :MLA_NOPE].reshape(DEPTH, MLA_KV_RANK, -1), MLA_HEADS, MLA_NOPE, 0)
    wv = _head_slabs(kv[..., MLA_NOPE:].reshape(DEPTH, MLA_KV_RANK, -1), MLA_HEADS, MLA_V, 0)

    return {
        "mix_g": norm_mix_g[:, None, :],
        "w_in": w_in_p,
        "vg": gmlp_v_norm_g[:, None, :],
        "ws": gmlp_w_s,
        "bsf": jnp.repeat(jnp.swapaxes(gmlp_b_s, 1, 2), GMLP_HEAD_DIM, axis=-1),
        "cw": ssd_conv_w,
        "cb": ssd_conv_b[:, None, :],
        "dtb": _pad_lanes(ssd_dt_bias, LANES)[:, None, :],
        "alog": _pad_lanes(ssd_a_log, LANES)[:, None, :],
        "dsk": jnp.repeat(ssd_d, SSD_HEAD_DIM, axis=-1)[:, None, :],
        "ng": ssd_norm_g[:, None, :],
        "qg": mla_q_norm_g[:, None, :],
        "kvg": mla_kv_norm_g[:, None, :],
        "wq": wq.astype(BF16),
        "wk": wk.astype(BF16),
        "wvt": jnp.swapaxes(wv, 1, 2).astype(BF16),
        "wo": w_out.astype(BF16),
        "mlp_g": norm_mlp_g[:, None, :],
        "w1": mlp_w1.astype(BF16),
        "w2": mlp_w2.astype(BF16),
    }


def _constant_tables():
    heads = np.arange(SSD_W) // SSD_HEAD_DIM
    expand_heads = (np.arange(LANES)[:, None] == heads[None, :]).astype(np.float32)
    groups = np.arange(SSD_W) // (SSD_W // SSD_GROUPS)
    state_mask = ((np.arange(SSD_BC) // SSD_STATE)[:, None] == groups[None, :]).astype(np.float32)
    return {
        "expand_heads2": jnp.asarray(np.concatenate([expand_heads, expand_heads], axis=0), dtype=BF16),
        "state_mask": jnp.asarray(state_mask),
    }


def _rope_tables(S):
    half = MLA_ROPE // 2
    pos = jnp.arange(S, dtype=F32)
    inv_freq = jnp.power(ROPE_BASE, -jnp.arange(half, dtype=F32) / half)
    ang = pos[:, None] * inv_freq[None, :]
    cos, sin = jnp.cos(ang), jnp.sin(ang)
    zeros = jnp.zeros((S, half), F32)
    pad = jnp.zeros((S, HEAD_SLAB - ROPE_LANE0 - MLA_ROPE), F32)
    ct = jnp.concatenate([jnp.ones((S, ROPE_LANE0), F32), cos, cos, pad], axis=1)
    s1 = jnp.concatenate([jnp.zeros((S, ROPE_LANE0), F32), -sin, zeros, pad], axis=1)
    s2 = jnp.concatenate([jnp.zeros((S, ROPE_LANE0), F32), zeros, sin, pad], axis=1)
    return ct, s1, s2


def kernel(x, norm_mix_g, w_in, gmlp_v_norm_g, gmlp_w_s, gmlp_b_s, ssd_conv_w, ssd_conv_b, ssd_dt_bias, ssd_a_log, ssd_d, ssd_norm_g, mla_q_norm_g, mla_w_qb, mla_kv_norm_g, mla_w_kvb, w_out, norm_mlp_g, mlp_w1, mlp_w2, final_norm_g):
    B, S, _ = x.shape
    assert B == 1 and S % TQ_ATT == 0 and S % TM_PROJ == 0 and S % TM_MLP == 0
    p = _layout_params(norm_mix_g, w_in, gmlp_v_norm_g, gmlp_w_s, gmlp_b_s, ssd_conv_w, ssd_conv_b,
                       ssd_dt_bias, ssd_a_log, ssd_d, ssd_norm_g, mla_q_norm_g, mla_w_qb,
                       mla_kv_norm_g, mla_w_kvb, w_out, norm_mlp_g, mlp_w1, mlp_w2)
    p.update(_constant_tables())
    tabs = _rope_tables(S)
    fg = final_norm_g[None, :]
    xs = x.reshape(S, D_MODEL)
    for l in range(DEPTH):
        uv, z, xbc, cq, ckv, dt, kr = _inproj(xs, p["mix_g"], p["w_in"], l)
        ya, yb = _mixab(uv, z, xbc, dt, p, l)
        q, k, vt = _mlaproj(cq, ckv, kr, p, l, tabs)
        yc = _flash(q, k, vt)
        xs = _outmlp(xs, ya, yb, yc, p, l, fg, final=(l == DEPTH - 1))
    return xs.reshape(B, S, D_MODEL)
```
